```python
import jax, jax.numpy as jnp
from jax import lax
import numpy as np

D_MODEL = 1024
BATCH = 2
SEQ = 8192
DEPTH = 1

CHUNK = 64
POOL_WIDTH = D_MODEL // 2
POOL_WINDOWS = (2, 4, 8, 16)
N_POOL_GROUPS = len(POOL_WINDOWS)
POOL_GROUP = POOL_WIDTH // N_POOL_GROUPS
SB_WIDTH = D_MODEL - POOL_WIDTH
SB_HEAD_DIM = 64
SB_HEADS = SB_WIDTH // SB_HEAD_DIM
Q_BLOCK = 128
D_FF = 4 * D_MODEL
PLE_DIM = 256
IN_WIDTH = POOL_WIDTH + 3 * SB_WIDTH
EPS = 1e-6

kernel_name = "hybrid_pool_stickbreaking_block"


def rms_norm(x, g):
    xf = x.astype(jnp.float32)
    y = xf * lax.rsqrt(jnp.mean(xf * xf, axis=-1, keepdims=True) + EPS)
    return (y * g.astype(jnp.float32)).astype(x.dtype)


def pool_mixer(u, w_pool, scale):
    b, s, _ = u.shape
    uf = u.astype(jnp.float32).reshape(b, s, N_POOL_GROUPS, POOL_GROUP)
    t = jnp.arange(s)
    outs = []
    for gi, w in enumerate(POOL_WINDOWS):
        ug = uf[:, :, gi]
        c = jnp.cumsum(ug, axis=1)
        c_prev = jnp.pad(c, ((0, 0), (w, 0), (0, 0)))[:, :s]
        cnt = jnp.minimum(t + 1, w).astype(jnp.float32)[None, :, None]
        outs.append((c - c_prev) / cnt - ug)
    d = jnp.stack(outs, axis=2).astype(u.dtype)
    y = jnp.einsum('bsgc,gcd->bsgd', d, w_pool).reshape(b, s, POOL_WIDTH)
    return y * scale


def stick_breaking_attention(q, k, v):
    b, s_len, h, dh = q.shape
    scale = dh ** -0.5
    outs = []
    for start in range(0, s_len, Q_BLOCK):
        end = start + Q_BLOCK
        qb = q[:, start:end]
        kb = k[:, :end]
        vb = v[:, :end]
        z = jnp.einsum('bqhd,bkhd->bhqk', qb, kb,
                       preferred_element_type=jnp.float32) * scale
        t_idx = start + jnp.arange(Q_BLOCK)
        s_idx = jnp.arange(end)
        mask = s_idx[None, :] < t_idx[:, None]
        log_fail = jnp.where(mask, jax.nn.log_sigmoid(-z), 0.0)
        after = lax.cumsum(log_fail, axis=3, reverse=True) - log_fail
        log_a = jax.nn.log_sigmoid(z) + after
        a = jnp.where(mask, jnp.exp(log_a), 0.0)
        outs.append(jnp.einsum('bhqk,bkhd->bqhd', a.astype(v.dtype), vb))
    return jnp.concatenate(outs, axis=1)


def setup_inputs(seed: int = 0) -> dict:
    key = jax.random.key(seed)
    ks = jax.random.split(key, 20)
    f32 = jnp.float32

    def nrm(k, shape, fan_in):
        return jax.random.normal(k, shape, f32) * (fan_in ** -0.5)

    def gain(k, shape):
        return 1.0 + 0.05 * jax.random.normal(k, shape, f32)

    return {
        "x": jax.random.normal(ks[0], (BATCH, SEQ, D_MODEL), f32),
        "p": jax.random.normal(ks[1], (DEPTH, BATCH, SEQ, PLE_DIM), f32),
        "g_mix_pre": gain(ks[2], (DEPTH, D_MODEL)),
        "w_in": nrm(ks[3], (DEPTH, D_MODEL, IN_WIDTH), D_MODEL),
        "w_pool": nrm(ks[4], (DEPTH, N_POOL_GROUPS, POOL_GROUP, POOL_GROUP), POOL_GROUP),
        "pool_scale": gain(ks[5], (DEPTH, POOL_WIDTH)),
        "g_sb": gain(ks[6], (DEPTH, SB_WIDTH)),
        "w_out": nrm(ks[7], (DEPTH, D_MODEL, D_MODEL), D_MODEL),
        "g_mix_post": gain(ks[8], (DEPTH, D_MODEL)),
        "g_mlp_pre": gain(ks[9], (DEPTH, D_MODEL)),
        "w_up": nrm(ks[10], (DEPTH, D_MODEL, D_FF), D_MODEL),
        "w_down": nrm(ks[11], (DEPTH, D_FF, D_MODEL), D_FF),
        "g_mlp_post": gain(ks[12], (DEPTH, D_MODEL)),
        "w_ple_gate": nrm(ks[13], (DEPTH, D_MODEL, D_MODEL), D_MODEL),
        "w_ple_proj": nrm(ks[14], (DEPTH, PLE_DIM, D_MODEL), PLE_DIM),
        "g_ple": gain(ks[15], (DEPTH, D_MODEL)),
    }


def reference(x, p, g_mix_pre, w_in, w_pool, pool_scale, g_sb, w_out, g_mix_post,
              g_mlp_pre, w_up, w_down, g_mlp_post, w_ple_gate, w_ple_proj, g_ple):
    b, s, _ = x.shape
    assert s % CHUNK == 0
    h = x
    for i in range(DEPTH):
        hn = rms_norm(h, g_mix_pre[i])
        proj = hn @ w_in[i]
        u = proj[..., :POOL_WIDTH]
        q, k, v = jnp.split(proj[..., POOL_WIDTH:], 3, axis=-1)
        q = q.reshape(b, s, SB_HEADS, SB_HEAD_DIM)
        k = k.reshape(b, s, SB_HEADS, SB_HEAD_DIM)
        v = v.reshape(b, s, SB_HEADS, SB_HEAD_DIM)

        y_pool = pool_mixer(u, w_pool[i], pool_scale[i])
        o_sb = stick_breaking_attention(q, k, v)
        y_sb = rms_norm(o_sb, jnp.ones((SB_HEAD_DIM,), jnp.float32)).reshape(b, s, SB_WIDTH) * g_sb[i]

        mix = jnp.concatenate([y_pool, y_sb], axis=-1) @ w_out[i]
        h = h + rms_norm(mix, g_mix_post[i])

        m = rms_norm(h, g_mlp_pre[i]) @ w_up[i]
        m = jnp.square(jax.nn.relu(m)) @ w_down[i]
        h = h + rms_norm(m, g_mlp_post[i])

        gate = jax.nn.sigmoid(h @ w_ple_gate[i])
        e = rms_norm(p[i] @ w_ple_proj[i], g_ple[i])
        h = h + gate * e
    return h
```

```python
import functools

import jax
import jax.numpy as jnp
from jax import lax
from jax.experimental import pallas as pl
from jax.experimental.pallas import tpu as pltpu

F32 = jnp.float32
BF16 = jnp.bfloat16

EPS = 1e-6
POOL_WINDOWS = (2, 4, 8, 16)
HEAD_DIM = 64
Q_SCALE = HEAD_DIM ** -0.5

LANES = 128
HEADS_PER_TILE = LANES // HEAD_DIM

ROW_TILE = 512
POOL_HALO = 16
TQ = 256
TK = 256
STICK_EXHAUSTED = 104.0

_NT_DIMS = (((1,), (1,)), ((), ()))


def _dot(a, b):
    return jnp.dot(a, b, preferred_element_type=F32)


def _rms_norm(x, g):
    ms = jnp.mean(x * x, axis=-1, keepdims=True)
    return x * lax.rsqrt(ms + EPS) * g


def _inproj_pool_kernel(x_ref, g_ref, wu_ref, wq_ref, wkt_ref, wv_ref, wpool_ref, pscale_ref,
                        ypool_ref, q_ref, kt_ref, v_ref, uext_ref):
    s = pl.program_id(1)
    rows = x_ref.shape[0]
    hn = _rms_norm(x_ref[...], g_ref[...]).astype(BF16)

    q_ref[...] = (_dot(hn, wq_ref[...]) * Q_SCALE).astype(BF16)
    v_ref[...] = _dot(hn, wv_ref[...]).astype(BF16)
    kt = lax.dot_general(wkt_ref[...], hn, _NT_DIMS, preferred_element_type=F32).astype(BF16)
    for hp in range(kt_ref.shape[0]):
        for kb in range(kt_ref.shape[1]):
            kt_ref[hp, kb] = kt[hp * LANES:(hp + 1) * LANES, kb * TK:(kb + 1) * TK]

    @pl.when(s == 0)
    def _():
        uext_ref[0:POOL_HALO, :] = jnp.zeros((POOL_HALO, uext_ref.shape[1]), F32)

    uext_ref[POOL_HALO:POOL_HALO + rows, :] = _dot(hn, wu_ref[...])
    t = s * rows + lax.broadcasted_iota(jnp.int32, (rows, 1), 0)
    for gi, w in enumerate(POOL_WINDOWS):
        cols = slice(gi * LANES, (gi + 1) * LANES)
        ug = uext_ref[POOL_HALO:POOL_HALO + rows, cols]
        acc = ug
        for i in range(1, w):
            acc = acc + uext_ref[POOL_HALO - i:POOL_HALO - i + rows, cols]
        cnt = jnp.minimum(t + 1, w).astype(F32)
        d = acc / cnt - ug
        y = _dot(d.astype(BF16), wpool_ref[gi]) * pscale_ref[:, cols]
        ypool_ref[:, cols] = y.astype(BF16)
    uext_ref[0:POOL_HALO, :] = uext_ref[rows:rows + POOL_HALO, :]


def _inproj_pool(x, g, wu, wq, wkt, wv, wpool, pscale):
    b, s, d = x.shape
    pw = wu.shape[1]
    sw = wq.shape[1]
    n_pairs = sw // LANES
    kb_per_tile = ROW_TILE // TK
    const2 = lambda bi, si: (0, 0)
    row_blk = lambda width: pl.BlockSpec((None, ROW_TILE, width), lambda bi, si: (bi, si, 0))
    return pl.pallas_call(
        _inproj_pool_kernel,
        grid=(b, s // ROW_TILE),
        in_specs=[
            row_blk(d),
            pl.BlockSpec((1, d), const2),
            pl.BlockSpec((d, pw), const2),
            pl.BlockSpec((d, sw), const2),
            pl.BlockSpec((sw, d), const2),
            pl.BlockSpec((d, sw), const2),
            pl.BlockSpec(wpool.shape, lambda bi, si: (0, 0, 0)),
            pl.BlockSpec((1, pw), const2),
        ],
        out_specs=[
            row_blk(pw),
            row_blk(sw),
            pl.BlockSpec((None, n_pairs, kb_per_tile, LANES, TK), lambda bi, si: (bi, 0, si, 0, 0)),
            row_blk(sw),
        ],
        out_shape=[
            jax.ShapeDtypeStruct((b, s, pw), BF16),
            jax.ShapeDtypeStruct((b, s, sw), BF16),
            jax.ShapeDtypeStruct((b, n_pairs, s // TK, LANES, TK), BF16),
            jax.ShapeDtypeStruct((b, s, sw), BF16),
        ],
        scratch_shapes=[pltpu.VMEM((ROW_TILE + POOL_HALO, pw), F32)],
        compiler_params=pltpu.CompilerParams(
            dimension_semantics=("arbitrary", "arbitrary"),
            vmem_limit_bytes=40 * 1024 * 1024),
        name="inproj_pool",
    )(x, g, wu, wq, wkt, wv, wpool, pscale)


def _attn_kernel(q_ref, kt_ref, v_ref, g_ref, o_ref, acc_ref, carry_ref):
    i = pl.program_id(2)
    q = q_ref[...]
    lane = lax.broadcasted_iota(jnp.int32, q.shape, 1)
    first_head = lane < HEAD_DIM
    zero = jnp.zeros_like(q)
    q_heads = (jnp.where(first_head, q, zero), jnp.where(first_head, zero, q))

    r = lax.broadcasted_iota(jnp.int32, (TK, TK), 0)
    c = lax.broadcasted_iota(jnp.int32, (TK, TK), 1)
    tri = jnp.where(r >= c, 1.0, 0.0).astype(BF16)
    tri2 = jnp.concatenate([tri, tri], axis=0)
    causal = c < r

    def sweep_block(j, diagonal):
        kt_blk = kt_ref[j]
        v_blk = v_ref[j]
        for h in range(HEADS_PER_TILE):
            z = _dot(q_heads[h], kt_blk)
            sp = jnp.maximum(z, 0.0) + jnp.log(1.0 + jnp.exp(-jnp.abs(z)))
            if diagonal:
                sp = jnp.where(causal, sp, 0.0)
            hi = sp.astype(BF16)
            lo = (sp - hi.astype(F32)).astype(BF16)
            cin = _dot(jnp.concatenate([hi, lo], axis=1), tri2)
            if diagonal:
                a = jnp.where(causal, jnp.exp(z - cin), 0.0)
                acc_ref[h] = _dot(a.astype(BF16), v_blk)
                carry_ref[h] = cin[:, 0:1]
            else:
                carry = carry_ref[h]
                a = jnp.exp(z - cin - carry)
                acc_ref[h] += _dot(a.astype(BF16), v_blk)
                carry_ref[h] = carry + cin[:, 0:1]

    sweep_block(i, True)

    @pl.when(i >= 1)
    def _():
        sweep_block(i - 1, False)

    @pl.when(i >= 2)
    def _():
        def unfinished(state):
            j, min_carry = state
            return jnp.logical_and(j >= 0, min_carry <= STICK_EXHAUSTED)

        def step(state):
            j, _ = state
            sweep_block(j, False)
            return j - 1, jnp.min(carry_ref[...])

        lax.while_loop(unfinished, step, (i - 2, jnp.min(carry_ref[...])))

    o = jnp.where(first_head, acc_ref[0], acc_ref[1])
    o2 = o * o
    ms0 = jnp.sum(jnp.where(first_head, o2, 0.0), axis=-1, keepdims=True) * (1.0 / HEAD_DIM)
    ms1 = jnp.sum(jnp.where(first_head, 0.0, o2), axis=-1, keepdims=True) * (1.0 / HEAD_DIM)
    inv = jnp.where(first_head, lax.rsqrt(ms0 + EPS), lax.rsqrt(ms1 + EPS))
    o_ref[...] = (o * inv * g_ref[...]).astype(o_ref.dtype)


def _sb_attention(q, kt, v, g_sb):
    b, s, sw = q.shape
    n_pairs = sw // LANES
    n_kb = s // TK
    v_blocks = v.reshape(b, n_kb, TK, sw)
    return pl.pallas_call(
        _attn_kernel,
        grid=(b, n_pairs, s // TQ),
        in_specs=[
            pl.BlockSpec((None, TQ, LANES), lambda bi, hp, i: (bi, i, hp)),
            pl.BlockSpec((None, None, n_kb, LANES, TK), lambda bi, hp, i: (bi, hp, 0, 0, 0)),
            pl.BlockSpec((None, n_kb, TK, LANES), lambda bi, hp, i: (bi, 0, 0, hp)),
            pl.BlockSpec((1, LANES), lambda bi, hp, i: (0, hp)),
        ],
        out_specs=pl.BlockSpec((None, TQ, LANES), lambda bi, hp, i: (bi, i, hp)),
        out_shape=jax.ShapeDtypeStruct((b, s, sw), BF16),
        scratch_shapes=[
            pltpu.VMEM((HEADS_PER_TILE, TQ, LANES), F32),
            pltpu.VMEM((HEADS_PER_TILE, TQ, 1), F32),
        ],
        compiler_params=pltpu.CompilerParams(
            dimension_semantics=("arbitrary", "arbitrary", "arbitrary"),
            vmem_limit_bytes=32 * 1024 * 1024),
        name="sb_attn",
    )(q, kt, v_blocks, g_sb)


def _post_kernel(x_ref, ypool_ref, ysb_ref, p_ref, wout_ref, gpost_ref, gpre_ref, wup_ref, wdown_ref,
                 gmlp_ref, wgate_ref, wple_ref, gple_ref, o_ref):
    pw = ypool_ref.shape[1]
    mix = _dot(ypool_ref[...], wout_ref[0:pw, :]) + _dot(ysb_ref[...], wout_ref[pw:, :])
    h = x_ref[...] + _rms_norm(mix, gpost_ref[...])

    hn = _rms_norm(h, gpre_ref[...]).astype(BF16)
    d_model = wup_ref.shape[0]
    m = None
    for c0 in range(0, wup_ref.shape[1], d_model):
        up = _dot(hn, wup_ref[:, c0:c0 + d_model])
        act = jnp.square(jnp.maximum(up, 0.0)).astype(BF16)
        part = _dot(act, wdown_ref[c0:c0 + d_model, :])
        m = part if m is None else m + part
    h = h + _rms_norm(m, gmlp_ref[...])

    gate = jax.nn.sigmoid(_dot(h.astype(BF16), wgate_ref[...]))
    e = _rms_norm(_dot(p_ref[...].astype(BF16), wple_ref[...]), gple_ref[...])
    o_ref[...] = h + gate * e


def _post_mlp(x, ypool, ysb, p, wout, gpost, gpre, wup, wdown, gmlp, wgate, wple, gple):
    t, d = x.shape
    row_blk = lambda width: pl.BlockSpec((ROW_TILE, width), lambda i: (i, 0))
    resident = lambda a: pl.BlockSpec(a.shape, lambda i: (0, 0), pipeline_mode=pl.Buffered(1))
    return pl.pallas_call(
        _post_kernel,
        grid=(t // ROW_TILE,),
        in_specs=[
            row_blk(d), row_blk(ypool.shape[1]), row_blk(ysb.shape[1]), row_blk(p.shape[1]),
            resident(wout), resident(gpost), resident(gpre), resident(wup), resident(wdown),
            resident(gmlp), resident(wgate), resident(wple), resident(gple),
        ],
        out_specs=row_blk(d),
        out_shape=jax.ShapeDtypeStruct((t, d), F32),
        compiler_params=pltpu.CompilerParams(
            dimension_semantics=("arbitrary",),
            vmem_limit_bytes=56 * 1024 * 1024),
        name="post_mlp",
    )(x, ypool, ysb, p, wout, gpost, gpre, wup, wdown, gmlp, wgate, wple, gple)


@jax.jit
def kernel(x, p, g_mix_pre, w_in, w_pool, pool_scale, g_sb, w_out, g_mix_post, g_mlp_pre, w_up,
           w_down, g_mlp_post, w_ple_gate, w_ple_proj, g_ple):
    b, s, d = x.shape
    depth = w_in.shape[0]
    pw = pool_scale.shape[1]
    sw = g_sb.shape[1]
    row = lambda a: a.reshape(1, -1)
    h = x
    for i in range(depth):
        w = w_in[i].astype(BF16)
        wu, wq, wk, wv = w[:, :pw], w[:, pw:pw + sw], w[:, pw + sw:pw + 2 * sw], w[:, pw + 2 * sw:]
        ypool, q, kt, v = _inproj_pool(h, row(g_mix_pre[i]), wu, wq, wk.T, wv,
                                       w_pool[i].astype(BF16), row(pool_scale[i]))
        ysb = _sb_attention(q, kt, v, row(g_sb[i]))
        t = b * s
        h = _post_mlp(
            h.reshape(t, d), ypool.reshape(t, pw), ysb.reshape(t, sw), p[i].reshape(t, -1),
            w_out[i].astype(BF16), row(g_mix_post[i]), row(g_mlp_pre[i]), w_up[i].astype(BF16),
            w_down[i].astype(BF16), row(g_mlp_post[i]), w_ple_gate[i].astype(BF16),
            w_ple_proj[i].astype(BF16), row(g_ple[i])).reshape(b, s, d)
    return h
```

```python
import jax
import jax.numpy as jnp
from jax import lax
from jax.experimental import pallas as pl
from jax.experimental.pallas import tpu as pltpu

F32 = jnp.float32
BF16 = jnp.bfloat16

EPS = 1e-6
POOL_WINDOWS = (2, 4, 8, 16)
HEAD_DIM = 64
Q_SCALE = HEAD_DIM ** -0.5
LOG2E = 1.4426950408889634

LANES = 128
HEADS_PER_TILE = LANES // HEAD_DIM

ROW_TILE = 512
POOL_HALO = 16
TQ = 256
TK = 256
ROW_CHUNK = 128
PAIRS_PER_STEP = 4
PIPE_LAG = 3
STICK_EXHAUSTED = 104.0
MASK_BIAS = -1e30

_NT_DIMS = (((1,), (1,)), ((), ()))


def _dot(a, b):
    return jnp.dot(a, b, preferred_element_type=F32)


def _rms_norm(x, g):
    ms = jnp.mean(x * x, axis=-1, keepdims=True)
    return x * lax.rsqrt(ms + EPS) * g


def _inproj_pool_kernel(x_ref, g_ref, wu_ref, wq_ref, wkt_ref, wv_ref, wpool_ref, pscale_ref,
                        ypool_ref, q_ref, kt_ref, v_ref, uext_ref):
    s = pl.program_id(1)
    rows = x_ref.shape[0]
    hn = _rms_norm(x_ref[...], g_ref[...]).astype(BF16)

    q_ref[...] = (_dot(hn, wq_ref[...]) * Q_SCALE).astype(BF16)
    v_ref[...] = _dot(hn, wv_ref[...]).astype(BF16)
    kt = lax.dot_general(wkt_ref[...], hn, _NT_DIMS, preferred_element_type=F32).astype(BF16)
    for hp in range(kt_ref.shape[0]):
        for kb in range(kt_ref.shape[1]):
            kt_ref[hp, kb] = kt[hp * LANES:(hp + 1) * LANES, kb * TK:(kb + 1) * TK]

    @pl.when(s == 0)
    def _():
        uext_ref[0:POOL_HALO, :] = jnp.zeros((POOL_HALO, uext_ref.shape[1]), F32)

    uext_ref[POOL_HALO:POOL_HALO + rows, :] = _dot(hn, wu_ref[...])
    t = s * rows + lax.broadcasted_iota(jnp.int32, (rows, 1), 0)
    for gi, w in enumerate(POOL_WINDOWS):
        cols = slice(gi * LANES, (gi + 1) * LANES)
        ug = uext_ref[POOL_HALO:POOL_HALO + rows, cols]
        acc = ug
        for i in range(1, w):
            acc = acc + uext_ref[POOL_HALO - i:POOL_HALO - i + rows, cols]
        cnt = jnp.minimum(t + 1, w).astype(F32)
        d = acc / cnt - ug
        y = _dot(d.astype(BF16), wpool_ref[gi]) * pscale_ref[:, cols]
        ypool_ref[:, cols] = y.astype(BF16)
    uext_ref[0:POOL_HALO, :] = uext_ref[rows:rows + POOL_HALO, :]


def _inproj_pool(x, g, wu, wq, wkt, wv, wpool, pscale):
    b, s, d = x.shape
    pw = wu.shape[1]
    sw = wq.shape[1]
    n_pairs = sw // LANES
    kb_per_tile = ROW_TILE // TK
    const2 = lambda bi, si: (0, 0)
    row_blk = lambda width: pl.BlockSpec((None, ROW_TILE, width), lambda bi, si: (bi, si, 0))
    return pl.pallas_call(
        _inproj_pool_kernel,
        grid=(b, s // ROW_TILE),
        in_specs=[
            row_blk(d),
            pl.BlockSpec((1, d), const2),
            pl.BlockSpec((d, pw), const2),
            pl.BlockSpec((d, sw), const2),
            pl.BlockSpec((sw, d), const2),
            pl.BlockSpec((d, sw), const2),
            pl.BlockSpec(wpool.shape, lambda bi, si: (0, 0, 0)),
            pl.BlockSpec((1, pw), const2),
        ],
        out_specs=[
            row_blk(pw),
            row_blk(sw),
            pl.BlockSpec((None, n_pairs, kb_per_tile, LANES, TK), lambda bi, si: (bi, 0, si, 0, 0)),
            row_blk(sw),
        ],
        out_shape=[
            jax.ShapeDtypeStruct((b, s, pw), BF16),
            jax.ShapeDtypeStruct((b, s, sw), BF16),
            jax.ShapeDtypeStruct((b, n_pairs, s // TK, LANES, TK), BF16),
            jax.ShapeDtypeStruct((b, s, sw), BF16),
        ],
        scratch_shapes=[pltpu.VMEM((ROW_TILE + POOL_HALO, pw), F32)],
        compiler_params=pltpu.CompilerParams(
            dimension_semantics=("arbitrary", "arbitrary"),
            vmem_limit_bytes=40 * 1024 * 1024),
        name="inproj_pool",
    )(x, g, wu, wq, wkt, wv, wpool, pscale)


def _softplus(z):
    return jnp.maximum(z, 0.0) + jnp.log(1.0 + jnp.exp2(jnp.abs(z) * (-LOG2E)))


def _cumulative_softplus(z, tri2):
    sp = _softplus(z)
    hi = sp.astype(BF16)
    lo = (sp - hi.astype(F32)).astype(BF16)
    return _dot(jnp.concatenate([hi, lo], axis=1), tri2)


def _head_rms_norm(acc_first, acc_second, gain, first_head):
    o = jnp.where(first_head, acc_first, acc_second)
    o2 = o * o
    ms0 = jnp.sum(jnp.where(first_head, o2, 0.0), axis=-1, keepdims=True) * (1.0 / HEAD_DIM)
    ms1 = jnp.sum(jnp.where(first_head, 0.0, o2), axis=-1, keepdims=True) * (1.0 / HEAD_DIM)
    inv = jnp.where(first_head, lax.rsqrt(ms0 + EPS), lax.rsqrt(ms1 + EPS))
    return (o * inv * gain).astype(BF16)


def _attn_kernel(q_ref, kt_ref, v_ref, g_ref, tri_ref, bias_ref, o_ref, acc_ref, carry_ref):
    i = pl.program_id(2)
    n_pairs = kt_ref.shape[0]

    def tri2(nk):
        tri = tri_ref[0:nk, 0:nk]
        return jnp.concatenate([tri, tri], axis=0)

    first_head = lax.broadcasted_iota(jnp.int32, (ROW_CHUNK, LANES), 1) < HEAD_DIM
    pair_lanes = lambda p: slice(p * LANES, (p + 1) * LANES)

    def q_head(p, h, rows):
        qc = q_ref[rows, pair_lanes(p)]
        zero = jnp.zeros_like(qc)
        return jnp.where(first_head, qc, zero) if h == 0 else jnp.where(first_head, zero, qc)

    j_prev = jnp.maximum(i - 1, 0)
    no_prev = jnp.where(i >= 1, 0.0, -MASK_BIAS)
    units = [(p, r0, h, blk) for p in range(n_pairs) for r0 in range(0, TQ, ROW_CHUNK)
             for h in range(HEADS_PER_TILE) for blk in ("diag", "prev")]
    qm, z, cin, part, carry, acc = {}, {}, {}, {}, {}, {}

    def n_keys(u):
        return u[1] + ROW_CHUNK if u[3] == "diag" else TK

    def scores(u):
        p, r0, h, blk = u
        rows = slice(r0, r0 + ROW_CHUNK)
        if blk == "diag":
            qm[p, r0, h] = q_head(p, h, rows)
            z[u] = _dot(qm[p, r0, h], kt_ref[p, i, :, 0:n_keys(u)]) + bias_ref[rows, 0:n_keys(u)]
        else:
            z[u] = _dot(qm[p, r0, h], kt_ref[p, j_prev])

    def stick(u):
        cin[u] = _cumulative_softplus(z[u], tri2(n_keys(u)))

    def weights_times_values(u):
        p, r0, h, blk = u
        rows = slice(r0, r0 + ROW_CHUNK)
        if blk == "diag":
            a = jnp.exp(z.pop(u) - cin[u]).astype(BF16)
            part[p, r0, h] = _dot(a, v_ref[i, 0:n_keys(u), pair_lanes(p)])
            carry[p, r0, h] = cin.pop(u)[:, 0:1]
            return
        c_in = carry.pop((p, r0, h))
        a = jnp.exp(z.pop(u) - cin[u] - (c_in + no_prev)).astype(BF16)
        acc[p, r0, h] = part.pop((p, r0, h)) + _dot(a, v_ref[j_prev, :, pair_lanes(p)])
        acc_ref[p, h, rows] = acc[p, r0, h]
        carry_ref[p, h, rows] = c_in + cin.pop(u)[:, 0:1]
        if h == HEADS_PER_TILE - 1:
            o_ref[rows, pair_lanes(p)] = _head_rms_norm(
                acc.pop((p, r0, 0)), acc.pop((p, r0, 1)), g_ref[:, pair_lanes(p)], first_head)

    stages = (scores, stick, weights_times_values)
    for t in range(len(units) + PIPE_LAG * (len(stages) - 1)):
        for k, stage in enumerate(stages):
            if 0 <= t - k * PIPE_LAG < len(units):
                stage(units[t - k * PIPE_LAG])

    @pl.when(jnp.logical_and(i >= 2, jnp.min(carry_ref[...]) <= STICK_EXHAUSTED))
    def _():
        def unfinished(state):
            j, min_carry = state
            return jnp.logical_and(j >= 0, min_carry <= STICK_EXHAUSTED)

        def step(state):
            j, _ = state
            for p in range(n_pairs):
                for h in range(HEADS_PER_TILE):
                    c_in = carry_ref[p, h]
                    qh = jnp.concatenate(
                        [q_head(p, h, slice(r0, r0 + ROW_CHUNK)) for r0 in range(0, TQ, ROW_CHUNK)], axis=0)
                    zj = _dot(qh, kt_ref[p, j])
                    cj = _cumulative_softplus(zj, tri2(TK))
                    a = jnp.exp(zj - cj - c_in).astype(BF16)
                    acc_ref[p, h] += _dot(a, v_ref[j, :, pair_lanes(p)])
                    carry_ref[p, h] = c_in + cj[:, 0:1]
            return j - 1, jnp.min(carry_ref[...])

        lax.while_loop(unfinished, step, (i - 2, jnp.float32(0.0)))
        for p in range(n_pairs):
            for r0 in range(0, TQ, ROW_CHUNK):
                rows = slice(r0, r0 + ROW_CHUNK)
                o_ref[rows, pair_lanes(p)] = _head_rms_norm(
                    acc_ref[p, 0, rows], acc_ref[p, 1, rows], g_ref[:, pair_lanes(p)], first_head)


def _sb_attention(q, kt, v, g_sb):
    b, s, sw = q.shape
    group_lanes = PAIRS_PER_STEP * LANES
    n_groups = sw // group_lanes
    n_kb = s // TK
    v_blocks = v.reshape(b, n_kb, TK, sw)
    r = lax.broadcasted_iota(jnp.int32, (TK, TK), 0)
    c = lax.broadcasted_iota(jnp.int32, (TK, TK), 1)
    tri = jnp.where(r >= c, 1.0, 0.0).astype(BF16)
    bias = jnp.where(c < r, 0.0, MASK_BIAS).astype(F32)
    const = lambda bi, gi, i: (0, 0)
    return pl.pallas_call(
        _attn_kernel,
        grid=(b, n_groups, s // TQ),
        in_specs=[
            pl.BlockSpec((None, TQ, group_lanes), lambda bi, gi, i: (bi, i, gi)),
            pl.BlockSpec((None, PAIRS_PER_STEP, n_kb, LANES, TK), lambda bi, gi, i: (bi, gi, 0, 0, 0)),
            pl.BlockSpec((None, n_kb, TK, group_lanes), lambda bi, gi, i: (bi, 0, 0, gi)),
            pl.BlockSpec((1, group_lanes), lambda bi, gi, i: (0, gi)),
            pl.BlockSpec((TK, TK), const),
            pl.BlockSpec((TQ, TK), const),
        ],
        out_specs=pl.BlockSpec((None, TQ, group_lanes), lambda bi, gi, i: (bi, i, gi)),
        out_shape=jax.ShapeDtypeStruct((b, s, sw), BF16),
        scratch_shapes=[
            pltpu.VMEM((PAIRS_PER_STEP, HEADS_PER_TILE, TQ, LANES), F32),
            pltpu.VMEM((PAIRS_PER_STEP, HEADS_PER_TILE, TQ, 1), F32),
        ],
        compiler_params=pltpu.CompilerParams(
            dimension_semantics=("arbitrary", "arbitrary", "arbitrary"),
            vmem_limit_bytes=48 * 1024 * 1024),
        name="sb_attn",
    )(q, kt, v_blocks, g_sb, tri, bias)


def _post_kernel(x_ref, ypool_ref, ysb_ref, p_ref, wout_ref, gpost_ref, gpre_ref, wup_ref, wdown_ref,
                 gmlp_ref, wgate_ref, wple_ref, gple_ref, o_ref):
    pw = ypool_ref.shape[1]
    mix = _dot(ypool_ref[...], wout_ref[0:pw, :]) + _dot(ysb_ref[...], wout_ref[pw:, :])
    h = x_ref[...] + _rms_norm(mix, gpost_ref[...])

    hn = _rms_norm(h, gpre_ref[...]).astype(BF16)
    d_model = wup_ref.shape[0]
    m = None
    for c0 in range(0, wup_ref.shape[1], d_model):
        up = _dot(hn, wup_ref[:, c0:c0 + d_model])
        act = jnp.square(jnp.maximum(up, 0.0)).astype(BF16)
        part = _dot(act, wdown_ref[c0:c0 + d_model, :])
        m = part if m is None else m + part
    h = h + _rms_norm(m, gmlp_ref[...])

    gate = jax.nn.sigmoid(_dot(h.astype(BF16), wgate_ref[...]))
    e = _rms_norm(_dot(p_ref[...].astype(BF16), wple_ref[...]), gple_ref[...])
    o_ref[...] = h + gate * e


def _post_mlp(x, ypool, ysb, p, wout, gpost, gpre, wup, wdown, gmlp, wgate, wple, gple):
    t, d = x.shape
    row_blk = lambda width: pl.BlockSpec((ROW_TILE, width), lambda i: (i, 0))
    resident = lambda a: pl.BlockSpec(a.shape, lambda i: (0, 0), pipeline_mode=pl.Buffered(1))
    return pl.pallas_call(
        _post_kernel,
        grid=(t // ROW_TILE,),
        in_specs=[
            row_blk(d), row_blk(ypool.shape[1]), row_blk(ysb.shape[1]), row_blk(p.shape[1]),
            resident(wout), resident(gpost), resident(gpre), resident(wup), resident(wdown),
            resident(gmlp), resident(wgate), resident(wple), resident(gple),
        ],
        out_specs=row_blk(d),
        out_shape=jax.ShapeDtypeStruct((t, d), F32),
        compiler_params=pltpu.CompilerParams(
            dimension_semantics=("arbitrary",),
            vmem_limit_bytes=56 * 1024 * 1024),
        name="post_mlp",
    )(x, ypool, ysb, p, wout, gpost, gpre, wup, wdown, gmlp, wgate, wple, gple)


@jax.jit
def kernel(x, p, g_mix_pre, w_in, w_pool, pool_scale, g_sb, w_out, g_mix_post, g_mlp_pre, w_up,
           w_down, g_mlp_post, w_ple_gate, w_ple_proj, g_ple):
    b, s, d = x.shape
    depth = w_in.shape[0]
    pw = pool_scale.shape[1]
    sw = g_sb.shape[1]
    row = lambda a: a.reshape(1, -1)
    h = x
    for i in range(depth):
        w = w_in[i].astype(BF16)
        wu, wq, wk, wv = w[:, :pw], w[:, pw:pw + sw], w[:, pw + sw:pw + 2 * sw], w[:, pw + 2 * sw:]
        ypool, q, kt, v = _inproj_pool(h, row(g_mix_pre[i]), wu, wq, wk.T, wv,
                                       w_pool[i].astype(BF16), row(pool_scale[i]))
        ysb = _sb_attention(q, kt, v, row(g_sb[i]))
        t = b * s
        h = _post_mlp(
            h.reshape(t, d), ypool.reshape(t, pw), ysb.reshape(t, sw), p[i].reshape(t, -1),
            w_out[i].astype(BF16), row(g_mix_post[i]), row(g_mlp_pre[i]), w_up[i].astype(BF16),
            w_down[i].astype(BF16), row(g_mlp_post[i]), w_ple_gate[i].astype(BF16),
            w_ple_proj[i].astype(BF16), row(g_ple[i])).reshape(b, s, d)
    return h
```

```python
import jax
import jax.numpy as jnp
from jax import lax
from jax.experimental import pallas as pl
from jax.experimental.pallas import tpu as pltpu

F32 = jnp.float32
BF16 = jnp.bfloat16

EPS = 1e-6
POOL_WINDOWS = (2, 4, 8, 16)
HEAD_DIM = 64
Q_SCALE = HEAD_DIM ** -0.5
LOG2E = 1.4426950408889634

LANES = 128
HEADS_PER_TILE = LANES // HEAD_DIM

ROW_TILE = 512
SUB_TILE = 256
POOL_HALO = 16
TQ = 256
TK = 256
ROW_CHUNK = 128
PAIRS_PER_STEP = 4
PIPE_LAG = 3
STICK_EXHAUSTED = 104.0
MASK_BIAS = -1e30

_NT_DIMS = (((1,), (1,)), ((), ()))


def _dot(a, b):
    return jnp.dot(a, b, preferred_element_type=F32)


def _rms_norm(x, g):
    ms = jnp.mean(x * x, axis=-1, keepdims=True)
    return x * lax.rsqrt(ms + EPS) * g


def _inproj_pool_kernel(x_ref, g_ref, wu_ref, wq_ref, wkt_ref, wv_ref, wpool_ref, pscale_ref,
                        ypool_ref, q_ref, kt_ref, v_ref, uext_ref):
    s = pl.program_id(1)
    rows = x_ref.shape[0]
    n_sub = rows // SUB_TILE
    sub_rows = lambda k: slice(k * SUB_TILE, (k + 1) * SUB_TILE)
    n_groups = len(POOL_WINDOWS)

    @pl.when(s == 0)
    def _():
        uext_ref[0:POOL_HALO, :] = jnp.zeros((POOL_HALO, uext_ref.shape[1]), F32)

    def normed(k):
        return _rms_norm(x_ref[sub_rows(k), :], g_ref[...]).astype(BF16)

    def project_u(k, hn):
        uext_ref[POOL_HALO + k * SUB_TILE:POOL_HALO + (k + 1) * SUB_TILE, :] = _dot(hn, wu_ref[...])

    def project_q(k, hn):
        q_ref[sub_rows(k), :] = (_dot(hn, wq_ref[...]) * Q_SCALE).astype(BF16)

    def project_v(k, hn):
        v_ref[sub_rows(k), :] = _dot(hn, wv_ref[...]).astype(BF16)

    def project_kt(k, hn):
        kt = lax.dot_general(wkt_ref[...], hn, _NT_DIMS, preferred_element_type=F32).astype(BF16)
        for hp in range(kt_ref.shape[0]):
            for kb in range(SUB_TILE // TK):
                kt_ref[hp, k * (SUB_TILE // TK) + kb] = kt[hp * LANES:(hp + 1) * LANES, kb * TK:(kb + 1) * TK]

    def pool(k, groups):
        base = POOL_HALO + k * SUB_TILE
        t = s * rows + k * SUB_TILE + lax.broadcasted_iota(jnp.int32, (SUB_TILE, 1), 0)
        for gi in groups:
            w = POOL_WINDOWS[gi]
            cols = slice(gi * LANES, (gi + 1) * LANES)
            ug = uext_ref[base:base + SUB_TILE, cols]
            acc = ug
            for i in range(1, w):
                acc = acc + uext_ref[base - i:base - i + SUB_TILE, cols]
            cnt = jnp.minimum(t + 1, w).astype(F32)
            d = acc / cnt - ug
            y = _dot(d.astype(BF16), wpool_ref[gi]) * pscale_ref[:, cols]
            ypool_ref[sub_rows(k), cols] = y.astype(BF16)

    hn = {0: normed(0)}
    project_u(0, hn[0])
    for k in range(n_sub):
        if k + 1 < n_sub:
            hn[k + 1] = normed(k + 1)
        project_q(k, hn[k])
        pool(k, range(0, n_groups // 2))
        if k + 1 < n_sub:
            project_u(k + 1, hn[k + 1])
        project_v(k, hn[k])
        pool(k, range(n_groups // 2, n_groups))
        project_kt(k, hn.pop(k))
    uext_ref[0:POOL_HALO, :] = uext_ref[rows:rows + POOL_HALO, :]


def _inproj_pool(x, g, wu, wq, wkt, wv, wpool, pscale):
    b, s, d = x.shape
    pw = wu.shape[1]
    sw = wq.shape[1]
    n_pairs = sw // LANES
    kb_per_tile = ROW_TILE // TK
    const2 = lambda bi, si: (0, 0)
    row_blk = lambda width: pl.BlockSpec((None, ROW_TILE, width), lambda bi, si: (bi, si, 0))
    return pl.pallas_call(
        _inproj_pool_kernel,
        grid=(b, s // ROW_TILE),
        in_specs=[
            row_blk(d),
            pl.BlockSpec((1, d), const2),
            pl.BlockSpec((d, pw), const2),
            pl.BlockSpec((d, sw), const2),
            pl.BlockSpec((sw, d), const2),
            pl.BlockSpec((d, sw), const2),
            pl.BlockSpec(wpool.shape, lambda bi, si: (0, 0, 0)),
            pl.BlockSpec((1, pw), const2),
        ],
        out_specs=[
            row_blk(pw),
            row_blk(sw),
            pl.BlockSpec((None, n_pairs, kb_per_tile, LANES, TK), lambda bi, si: (bi, 0, si, 0, 0)),
            row_blk(sw),
        ],
        out_shape=[
            jax.ShapeDtypeStruct((b, s, pw), BF16),
            jax.ShapeDtypeStruct((b, s, sw), BF16),
            jax.ShapeDtypeStruct((b, n_pairs, s // TK, LANES, TK), BF16),
            jax.ShapeDtypeStruct((b, s, sw), BF16),
        ],
        scratch_shapes=[pltpu.VMEM((ROW_TILE + POOL_HALO, pw), F32)],
        compiler_params=pltpu.CompilerParams(
            dimension_semantics=("arbitrary", "arbitrary"),
            vmem_limit_bytes=40 * 1024 * 1024),
        name="inproj_pool",
    )(x, g, wu, wq, wkt, wv, wpool, pscale)


def _softplus(z):
    return jnp.maximum(z, 0.0) + jnp.log(1.0 + jnp.exp2(jnp.abs(z) * (-LOG2E)))


def _cumulative_softplus(z, tri):
    return _dot(_softplus(z).astype(BF16), tri)


def _head_rms_norm(acc_first, acc_second, gain, first_head):
    o = jnp.where(first_head, acc_first, acc_second)
    o2 = o * o
    ms0 = jnp.sum(jnp.where(first_head, o2, 0.0), axis=-1, keepdims=True) * (1.0 / HEAD_DIM)
    ms1 = jnp.sum(jnp.where(first_head, 0.0, o2), axis=-1, keepdims=True) * (1.0 / HEAD_DIM)
    inv = jnp.where(first_head, lax.rsqrt(ms0 + EPS), lax.rsqrt(ms1 + EPS))
    return (o * inv * gain).astype(BF16)


def _attn_kernel(q_ref, kt_ref, v_ref, g_ref, tri_ref, bias_ref, o_ref, acc_ref, carry_ref):
    i = pl.program_id(2)
    n_pairs = kt_ref.shape[0]

    def tri(nk):
        return tri_ref[0:nk, 0:nk]

    first_head = lax.broadcasted_iota(jnp.int32, (ROW_CHUNK, LANES), 1) < HEAD_DIM
    pair_lanes = lambda p: slice(p * LANES, (p + 1) * LANES)

    def q_head(p, h, rows):
        qc = q_ref[rows, pair_lanes(p)]
        zero = jnp.zeros_like(qc)
        return jnp.where(first_head, qc, zero) if h == 0 else jnp.where(first_head, zero, qc)

    j_prev = jnp.maximum(i - 1, 0)
    no_prev = jnp.where(i >= 1, 0.0, -MASK_BIAS)
    units = [(p, r0, h, blk) for p in range(n_pairs) for r0 in range(0, TQ, ROW_CHUNK)
             for h in range(HEADS_PER_TILE) for blk in ("diag", "prev")]
    qm, z, cin, part, carry, acc = {}, {}, {}, {}, {}, {}

    def n_keys(u):
        return u[1] + ROW_CHUNK if u[3] == "diag" else TK

    def scores(u):
        p, r0, h, blk = u
        rows = slice(r0, r0 + ROW_CHUNK)
        if blk == "diag":
            qm[p, r0, h] = q_head(p, h, rows)
            z[u] = _dot(qm[p, r0, h], kt_ref[p, i, :, 0:n_keys(u)]) + bias_ref[rows, 0:n_keys(u)]
        else:
            z[u] = _dot(qm[p, r0, h], kt_ref[p, j_prev])

    def stick(u):
        cin[u] = _cumulative_softplus(z[u], tri(n_keys(u)))

    def weights_times_values(u):
        p, r0, h, blk = u
        rows = slice(r0, r0 + ROW_CHUNK)
        if blk == "diag":
            a = jnp.exp(z.pop(u) - cin[u]).astype(BF16)
            part[p, r0, h] = _dot(a, v_ref[i, 0:n_keys(u), pair_lanes(p)])
            carry[p, r0, h] = cin.pop(u)[:, 0:1]
            return
        c_in = carry.pop((p, r0, h))
        a = jnp.exp(z.pop(u) - cin[u] - (c_in + no_prev)).astype(BF16)
        acc[p, r0, h] = part.pop((p, r0, h)) + _dot(a, v_ref[j_prev, :, pair_lanes(p)])
        acc_ref[p, h, rows] = acc[p, r0, h]
        carry_ref[p, h, rows] = c_in + cin.pop(u)[:, 0:1]
        if h == HEADS_PER_TILE - 1:
            o_ref[rows, pair_lanes(p)] = _head_rms_norm(
                acc.pop((p, r0, 0)), acc.pop((p, r0, 1)), g_ref[:, pair_lanes(p)], first_head)

    stages = (scores, stick, weights_times_values)
    for t in range(len(units) + PIPE_LAG * (len(stages) - 1)):
        for k, stage in enumerate(stages):
            if 0 <= t - k * PIPE_LAG < len(units):
                stage(units[t - k * PIPE_LAG])

    @pl.when(jnp.logical_and(i >= 2, jnp.min(carry_ref[...]) <= STICK_EXHAUSTED))
    def _():
        def unfinished(state):
            j, min_carry = state
            return jnp.logical_and(j >= 0, min_carry <= STICK_EXHAUSTED)

        def step(state):
            j, _ = state
            for p in range(n_pairs):
                for h in range(HEADS_PER_TILE):
                    c_in = carry_ref[p, h]
                    qh = jnp.concatenate(
                        [q_head(p, h, slice(r0, r0 + ROW_CHUNK)) for r0 in range(0, TQ, ROW_CHUNK)], axis=0)
                    zj = _dot(qh, kt_ref[p, j])
                    cj = _cumulative_softplus(zj, tri(TK))
                    a = jnp.exp(zj - cj - c_in).astype(BF16)
                    acc_ref[p, h] += _dot(a, v_ref[j, :, pair_lanes(p)])
                    carry_ref[p, h] = c_in + cj[:, 0:1]
            return j - 1, jnp.min(carry_ref[...])

        lax.while_loop(unfinished, step, (i - 2, jnp.float32(0.0)))
        for p in range(n_pairs):
            for r0 in range(0, TQ, ROW_CHUNK):
                rows = slice(r0, r0 + ROW_CHUNK)
                o_ref[rows, pair_lanes(p)] = _head_rms_norm(
                    acc_ref[p, 0, rows], acc_ref[p, 1, rows], g_ref[:, pair_lanes(p)], first_head)


def _sb_attention(q, kt, v, g_sb):
    b, s, sw = q.shape
    group_lanes = PAIRS_PER_STEP * LANES
    n_groups = sw // group_lanes
    n_kb = s // TK
    v_blocks = v.reshape(b, n_kb, TK, sw)
    r = lax.broadcasted_iota(jnp.int32, (TK, TK), 0)
    c = lax.broadcasted_iota(jnp.int32, (TK, TK), 1)
    tri = jnp.where(r >= c, 1.0, 0.0).astype(BF16)
    bias = jnp.where(c < r, 0.0, MASK_BIAS).astype(F32)
    const = lambda bi, gi, i: (0, 0)
    return pl.pallas_call(
        _attn_kernel,
        grid=(b, n_groups, s // TQ),
        in_specs=[
            pl.BlockSpec((None, TQ, group_lanes), lambda bi, gi, i: (bi, i, gi)),
            pl.BlockSpec((None, PAIRS_PER_STEP, n_kb, LANES, TK), lambda bi, gi, i: (bi, gi, 0, 0, 0)),
            pl.BlockSpec((None, n_kb, TK, group_lanes), lambda bi, gi, i: (bi, 0, 0, gi)),
            pl.BlockSpec((1, group_lanes), lambda bi, gi, i: (0, gi)),
            pl.BlockSpec((TK, TK), const),
            pl.BlockSpec((TQ, TK), const),
        ],
        out_specs=pl.BlockSpec((None, TQ, group_lanes), lambda bi, gi, i: (bi, i, gi)),
        out_shape=jax.ShapeDtypeStruct((b, s, sw), BF16),
        scratch_shapes=[
            pltpu.VMEM((PAIRS_PER_STEP, HEADS_PER_TILE, TQ, LANES), F32),
            pltpu.VMEM((PAIRS_PER_STEP, HEADS_PER_TILE, TQ, 1), F32),
        ],
        compiler_params=pltpu.CompilerParams(
            dimension_semantics=("arbitrary", "arbitrary", "arbitrary"),
            vmem_limit_bytes=48 * 1024 * 1024),
        name="sb_attn",
    )(q, kt, v_blocks, g_sb, tri, bias)


def _post_kernel(x_ref, ypool_ref, ysb_ref, p_ref, wout_ref, gpost_ref, gpre_ref, wup_ref, wdown_ref,
                 gmlp_ref, wgate_ref, wple_ref, gple_ref, o_ref):
    pw = ypool_ref.shape[1]
    d_model = wup_ref.shape[0]
    n_sub = x_ref.shape[0] // SUB_TILE
    sub_rows = lambda k: slice(k * SUB_TILE, (k + 1) * SUB_TILE)
    h, hn, m = {}, {}, {}

    def norms(k, mix):
        h[k] = x_ref[sub_rows(k), :] + _rms_norm(mix, gpost_ref[...])
        hn[k] = _rms_norm(h[k], gpre_ref[...]).astype(BF16)

    def mlp(k, after_first_up):
        acc = None
        for c0 in range(0, wup_ref.shape[1], d_model):
            up = _dot(hn[k], wup_ref[:, c0:c0 + d_model])
            if c0 == 0:
                after_first_up()
            act = jnp.square(jnp.maximum(up, 0.0)).astype(BF16)
            part = _dot(act, wdown_ref[c0:c0 + d_model, :])
            acc = part if acc is None else acc + part
        m[k] = acc

    def finish(k):
        h2 = h.pop(k) + _rms_norm(m.pop(k), gmlp_ref[...])
        gate = jax.nn.sigmoid(_dot(h2.astype(BF16), wgate_ref[...]))
        o_ref[sub_rows(k), :] = h2 + gate * e[k]

    def mix(k):
        return (_dot(ypool_ref[sub_rows(k), :], wout_ref[0:pw, :])
                + _dot(ysb_ref[sub_rows(k), :], wout_ref[pw:, :]))

    norms(0, mix(0))
    mixes = [None] + [mix(k) for k in range(1, n_sub)]
    e = [_rms_norm(_dot(p_ref[sub_rows(k), :].astype(BF16), wple_ref[...]), gple_ref[...])
         for k in range(n_sub)]
    for k in range(n_sub):
        def between(k=k):
            if k + 1 < n_sub:
                norms(k + 1, mixes[k + 1])
            if k >= 1:
                finish(k - 1)
        mlp(k, between)
    finish(n_sub - 1)


def _post_mlp(x, ypool, ysb, p, wout, gpost, gpre, wup, wdown, gmlp, wgate, wple, gple):
    t, d = x.shape
    row_blk = lambda width: pl.BlockSpec((ROW_TILE, width), lambda i: (i, 0))
    resident = lambda a: pl.BlockSpec(a.shape, lambda i: (0, 0), pipeline_mode=pl.Buffered(1))
    return pl.pallas_call(
        _post_kernel,
        grid=(t // ROW_TILE,),
        in_specs=[
            row_blk(d), row_blk(ypool.shape[1]), row_blk(ysb.shape[1]), row_blk(p.shape[1]),
            resident(wout), resident(gpost), resident(gpre), resident(wup), resident(wdown),
            resident(gmlp), resident(wgate), resident(wple), resident(gple),
        ],
        out_specs=row_blk(d),
        out_shape=jax.ShapeDtypeStruct((t, d), F32),
        compiler_params=pltpu.CompilerParams(
            dimension_semantics=("arbitrary",),
            vmem_limit_bytes=56 * 1024 * 1024),
        name="post_mlp",
    )(x, ypool, ysb, p, wout, gpost, gpre, wup, wdown, gmlp, wgate, wple, gple)


@jax.jit
def kernel(x, p, g_mix_pre, w_in, w_pool, pool_scale, g_sb, w_out, g_mix_post, g_mlp_pre, w_up,
           w_down, g_mlp_post, w_ple_gate, w_ple_proj, g_ple):
    b, s, d = x.shape
    depth = w_in.shape[0]
    pw = pool_scale.shape[1]
    sw = g_sb.shape[1]
    row = lambda a: a.reshape(1, -1)
    h = x
    for i in range(depth):
        w = w_in[i].astype(BF16)
        wu, wq, wk, wv = w[:, :pw], w[:, pw:pw + sw], w[:, pw + sw:pw + 2 * sw], w[:, pw + 2 * sw:]
        ypool, q, kt, v = _inproj_pool(h, row(g_mix_pre[i]), wu, wq, wk.T, wv,
                                       w_pool[i].astype(BF16), row(pool_scale[i]))
        ysb = _sb_attention(q, kt, v, row(g_sb[i]))
        t = b * s
        h = _post_mlp(
            h.reshape(t, d), ypool.reshape(t, pw), ysb.reshape(t, sw), p[i].reshape(t, -1),
            w_out[i].astype(BF16), row(g_mix_post[i]), row(g_mlp_pre[i]), w_up[i].astype(BF16),
            w_down[i].astype(BF16), row(g_mlp_post[i]), w_ple_gate[i].astype(BF16),
            w_ple_proj[i].astype(BF16), row(g_ple[i])).reshape(b, s, d)
    return h
```

```python
import jax
import jax.numpy as jnp
from jax import lax
from jax.experimental import pallas as pl
from jax.experimental.pallas import tpu as pltpu

F32 = jnp.float32
BF16 = jnp.bfloat16

EPS = 1e-6
POOL_WINDOWS = (2, 4, 8, 16)
HEAD_DIM = 64
Q_SCALE = HEAD_DIM ** -0.5
LOG2E = 1.4426950408889634

LANES = 128
HEADS_PER_TILE = LANES // HEAD_DIM

ROW_TILE = 1024
POST_ROW_TILE = 512
SUB_TILE = 256
WEIGHT_STAGE_ROWS = 256
POOL_HALO = 16
TQ = 256
TK = 256
ROW_CHUNK = 128
PAIRS_PER_STEP = 4
PIPE_LAG = 3
STICK_EXHAUSTED = 104.0
MASK_BIAS = -1e30

_NT_DIMS = (((1,), (1,)), ((), ()))


def _dot(a, b):
    return jnp.dot(a, b, preferred_element_type=F32)


def _rms_norm(x, g):
    ms = jnp.mean(x * x, axis=-1, keepdims=True)
    return x * lax.rsqrt(ms + EPS) * g


def _inproj_pool_kernel(x_ref, g_ref, wu_ref, wq_ref, wkt_ref, wv_ref, wpool_ref, pscale_ref,
                        ypool_ref, q_ref, kt_ref, v_ref, uext_ref):
    s = pl.program_id(1)
    rows = x_ref.shape[0]
    n_sub = rows // SUB_TILE
    sub_rows = lambda k: slice(k * SUB_TILE, (k + 1) * SUB_TILE)
    n_groups = len(POOL_WINDOWS)

    @pl.when(s == 0)
    def _():
        uext_ref[0:POOL_HALO, :] = jnp.zeros((POOL_HALO, uext_ref.shape[1]), F32)

    def normed(k):
        return _rms_norm(x_ref[sub_rows(k), :], g_ref[...]).astype(BF16)

    def project_u(k, hn):
        uext_ref[POOL_HALO + k * SUB_TILE:POOL_HALO + (k + 1) * SUB_TILE, :] = _dot(hn, wu_ref[...])

    def project_q(k, hn):
        q_ref[sub_rows(k), :] = (_dot(hn, wq_ref[...]) * Q_SCALE).astype(BF16)

    def project_v(k, hn):
        v_ref[sub_rows(k), :] = _dot(hn, wv_ref[...]).astype(BF16)

    def project_kt(k, hn):
        kt = lax.dot_general(wkt_ref[...], hn, _NT_DIMS, preferred_element_type=F32).astype(BF16)
        for hp in range(kt_ref.shape[0]):
            for kb in range(SUB_TILE // TK):
                kt_ref[hp, k * (SUB_TILE // TK) + kb] = kt[hp * LANES:(hp + 1) * LANES, kb * TK:(kb + 1) * TK]

    def pool(k, groups):
        base = POOL_HALO + k * SUB_TILE
        t = s * rows + k * SUB_TILE + lax.broadcasted_iota(jnp.int32, (SUB_TILE, 1), 0)
        for gi in groups:
            w = POOL_WINDOWS[gi]
            cols = slice(gi * LANES, (gi + 1) * LANES)
            ug = uext_ref[base:base + SUB_TILE, cols]
            acc = ug
            for i in range(1, w):
                acc = acc + uext_ref[base - i:base - i + SUB_TILE, cols]
            cnt = jnp.minimum(t + 1, w).astype(F32)
            d = acc / cnt - ug
            y = _dot(d.astype(BF16), wpool_ref[gi]) * pscale_ref[:, cols]
            ypool_ref[sub_rows(k), cols] = y.astype(BF16)

    hn = {0: normed(0)}
    project_u(0, hn[0])
    for k in range(n_sub):
        if k + 1 < n_sub:
            hn[k + 1] = normed(k + 1)
        project_q(k, hn[k])
        pool(k, range(0, n_groups // 2))
        if k + 1 < n_sub:
            project_u(k + 1, hn[k + 1])
        project_v(k, hn[k])
        pool(k, range(n_groups // 2, n_groups))
        project_kt(k, hn.pop(k))
    uext_ref[0:POOL_HALO, :] = uext_ref[rows:rows + POOL_HALO, :]


def _inproj_pool(x, g, wu, wq, wkt, wv, wpool, pscale):
    b, s, d = x.shape
    pw = wu.shape[1]
    sw = wq.shape[1]
    n_pairs = sw // LANES
    kb_per_tile = ROW_TILE // TK
    const2 = lambda bi, si: (0, 0)
    row_blk = lambda width: pl.BlockSpec((None, ROW_TILE, width), lambda bi, si: (bi, si, 0))
    return pl.pallas_call(
        _inproj_pool_kernel,
        grid=(b, s // ROW_TILE),
        in_specs=[
            row_blk(d),
            pl.BlockSpec((1, d), const2),
            pl.BlockSpec((d, pw), const2),
            pl.BlockSpec((d, sw), const2),
            pl.BlockSpec((sw, d), const2),
            pl.BlockSpec((d, sw), const2),
            pl.BlockSpec(wpool.shape, lambda bi, si: (0, 0, 0)),
            pl.BlockSpec((1, pw), const2),
        ],
        out_specs=[
            row_blk(pw),
            row_blk(sw),
            pl.BlockSpec((None, n_pairs, kb_per_tile, LANES, TK), lambda bi, si: (bi, 0, si, 0, 0)),
            row_blk(sw),
        ],
        out_shape=[
            jax.ShapeDtypeStruct((b, s, pw), BF16),
            jax.ShapeDtypeStruct((b, s, sw), BF16),
            jax.ShapeDtypeStruct((b, n_pairs, s // TK, LANES, TK), BF16),
            jax.ShapeDtypeStruct((b, s, sw), BF16),
        ],
        scratch_shapes=[pltpu.VMEM((ROW_TILE + POOL_HALO, pw), F32)],
        compiler_params=pltpu.CompilerParams(
            dimension_semantics=("arbitrary", "arbitrary"),
            vmem_limit_bytes=40 * 1024 * 1024),
        name="inproj_pool",
    )(x, g, wu, wq, wkt, wv, wpool, pscale)


def _softplus(z):
    return jnp.maximum(z, 0.0) + jnp.log(1.0 + jnp.exp2(jnp.abs(z) * (-LOG2E)))


def _cumulative_softplus(z, tri):
    return _dot(_softplus(z).astype(BF16), tri)


def _head_rms_norm(acc_first, acc_second, gain, first_head):
    o = jnp.where(first_head, acc_first, acc_second)
    o2 = o * o
    ms0 = jnp.sum(jnp.where(first_head, o2, 0.0), axis=-1, keepdims=True) * (1.0 / HEAD_DIM)
    ms1 = jnp.sum(jnp.where(first_head, 0.0, o2), axis=-1, keepdims=True) * (1.0 / HEAD_DIM)
    inv = jnp.where(first_head, lax.rsqrt(ms0 + EPS), lax.rsqrt(ms1 + EPS))
    return (o * inv * gain).astype(BF16)


def _attn_kernel(q_ref, kt_ref, v_ref, g_ref, tri_ref, bias_ref, o_ref, acc_ref, carry_ref):
    i = pl.program_id(2)
    n_pairs = kt_ref.shape[0]

    def tri(nk):
        return tri_ref[0:nk, 0:nk]

    first_head = lax.broadcasted_iota(jnp.int32, (ROW_CHUNK, LANES), 1) < HEAD_DIM
    pair_lanes = lambda p: slice(p * LANES, (p + 1) * LANES)

    def q_head(p, h, rows):
        qc = q_ref[rows, pair_lanes(p)]
        zero = jnp.zeros_like(qc)
        return jnp.where(first_head, qc, zero) if h == 0 else jnp.where(first_head, zero, qc)

    j_prev = jnp.maximum(i - 1, 0)
    no_prev = jnp.where(i >= 1, 0.0, -MASK_BIAS)
    units = [(p, r0, h, blk) for p in range(n_pairs) for r0 in range(0, TQ, ROW_CHUNK)
             for h in range(HEADS_PER_TILE) for blk in ("diag", "prev")]
    qm, z, cin, part, carry, acc = {}, {}, {}, {}, {}, {}

    def n_keys(u):
        return u[1] + ROW_CHUNK if u[3] == "diag" else TK - u[1]

    def scores(u):
        p, r0, h, blk = u
        rows = slice(r0, r0 + ROW_CHUNK)
        if blk == "diag":
            qm[p, r0, h] = q_head(p, h, rows)
            z[u] = _dot(qm[p, r0, h], kt_ref[p, i, :, 0:n_keys(u)]) + bias_ref[rows, 0:n_keys(u)]
        else:
            z[u] = _dot(qm[p, r0, h], kt_ref[p, j_prev, :, r0:TK])

    def stick(u):
        cin[u] = _cumulative_softplus(z[u], tri(n_keys(u)))

    def weights_times_values(u):
        p, r0, h, blk = u
        rows = slice(r0, r0 + ROW_CHUNK)
        if blk == "diag":
            a = jnp.exp(z.pop(u) - cin[u]).astype(BF16)
            part[p, r0, h] = _dot(a, v_ref[i, 0:n_keys(u), pair_lanes(p)])
            carry[p, r0, h] = cin.pop(u)[:, 0:1]
            return
        c_in = carry.pop((p, r0, h))
        a = jnp.exp(z.pop(u) - cin[u] - (c_in + no_prev)).astype(BF16)
        acc[p, r0, h] = part.pop((p, r0, h)) + _dot(a, v_ref[j_prev, r0:TK, pair_lanes(p)])
        acc_ref[p, h, rows] = acc[p, r0, h]
        carry_ref[p, h, rows] = c_in + cin.pop(u)[:, 0:1]
        if h == HEADS_PER_TILE - 1:
            o_ref[rows, pair_lanes(p)] = _head_rms_norm(
                acc.pop((p, r0, 0)), acc.pop((p, r0, 1)), g_ref[:, pair_lanes(p)], first_head)

    stages = (scores, stick, weights_times_values)
    for t in range(len(units) + PIPE_LAG * (len(stages) - 1)):
        for k, stage in enumerate(stages):
            if 0 <= t - k * PIPE_LAG < len(units):
                stage(units[t - k * PIPE_LAG])

    @pl.when(jnp.logical_and(i >= 1, jnp.min(carry_ref[...]) <= STICK_EXHAUSTED))
    def _():
        for p in range(n_pairs):
            for h in range(HEADS_PER_TILE):
                for r0 in range(ROW_CHUNK, TQ, ROW_CHUNK):
                    rows = slice(r0, r0 + ROW_CHUNK)
                    c_in = carry_ref[p, h, rows]
                    zj = _dot(q_head(p, h, rows), kt_ref[p, j_prev, :, 0:r0])
                    cj = _cumulative_softplus(zj, tri(r0))
                    a = jnp.exp(zj - cj - c_in).astype(BF16)
                    acc_ref[p, h, rows] += _dot(a, v_ref[j_prev, 0:r0, pair_lanes(p)])
                    carry_ref[p, h, rows] = c_in + cj[:, 0:1]

        def unfinished(state):
            j, min_carry = state
            return jnp.logical_and(j >= 0, min_carry <= STICK_EXHAUSTED)

        def step(state):
            j, _ = state
            for p in range(n_pairs):
                for h in range(HEADS_PER_TILE):
                    c_in = carry_ref[p, h]
                    qh = jnp.concatenate(
                        [q_head(p, h, slice(r0, r0 + ROW_CHUNK)) for r0 in range(0, TQ, ROW_CHUNK)], axis=0)
                    zj = _dot(qh, kt_ref[p, j])
                    cj = _cumulative_softplus(zj, tri(TK))
                    a = jnp.exp(zj - cj - c_in).astype(BF16)
                    acc_ref[p, h] += _dot(a, v_ref[j, :, pair_lanes(p)])
                    carry_ref[p, h] = c_in + cj[:, 0:1]
            return j - 1, jnp.min(carry_ref[...])

        lax.while_loop(unfinished, step, (i - 2, jnp.min(carry_ref[...])))
        for p in range(n_pairs):
            for r0 in range(0, TQ, ROW_CHUNK):
                rows = slice(r0, r0 + ROW_CHUNK)
                o_ref[rows, pair_lanes(p)] = _head_rms_norm(
                    acc_ref[p, 0, rows], acc_ref[p, 1, rows], g_ref[:, pair_lanes(p)], first_head)


def _sb_attention(q, kt, v, g_sb):
    b, s, sw = q.shape
    group_lanes = PAIRS_PER_STEP * LANES
    n_groups = sw // group_lanes
    n_kb = s // TK
    v_blocks = v.reshape(b, n_kb, TK, sw)
    r = lax.broadcasted_iota(jnp.int32, (TK, TK), 0)
    c = lax.broadcasted_iota(jnp.int32, (TK, TK), 1)
    tri = jnp.where(r >= c, 1.0, 0.0).astype(BF16)
    bias = jnp.where(c < r, 0.0, MASK_BIAS).astype(F32)
    const = lambda bi, gi, i: (0, 0)
    return pl.pallas_call(
        _attn_kernel,
        grid=(b, n_groups, s // TQ),
        in_specs=[
            pl.BlockSpec((None, TQ, group_lanes), lambda bi, gi, i: (bi, i, gi)),
            pl.BlockSpec((None, PAIRS_PER_STEP, n_kb, LANES, TK), lambda bi, gi, i: (bi, gi, 0, 0, 0)),
            pl.BlockSpec((None, n_kb, TK, group_lanes), lambda bi, gi, i: (bi, 0, 0, gi)),
            pl.BlockSpec((1, group_lanes), lambda bi, gi, i: (0, gi)),
            pl.BlockSpec((TK, TK), const),
            pl.BlockSpec((TQ, TK), const),
        ],
        out_specs=pl.BlockSpec((None, TQ, group_lanes), lambda bi, gi, i: (bi, i, gi)),
        out_shape=jax.ShapeDtypeStruct((b, s, sw), BF16),
        scratch_shapes=[
            pltpu.VMEM((PAIRS_PER_STEP, HEADS_PER_TILE, TQ, LANES), F32),
            pltpu.VMEM((PAIRS_PER_STEP, HEADS_PER_TILE, TQ, 1), F32),
        ],
        compiler_params=pltpu.CompilerParams(
            dimension_semantics=("arbitrary", "arbitrary", "arbitrary"),
            vmem_limit_bytes=48 * 1024 * 1024),
        name="sb_attn",
    )(q, kt, v_blocks, g_sb, tri, bias)


def _load_weights_as_bf16(pairs, stage_ref, sem_ref):
    stage_rows, stage_cols = stage_ref.shape[1:]
    chunks = []
    for src, dst in pairs:
        n_rows, n_cols = src.shape
        for r0 in range(0, n_rows, stage_rows):
            rows = min(stage_rows, n_rows - r0)
            for c0 in range(0, n_cols, stage_cols):
                chunks.append((src, dst, r0, rows, c0))

    def copy(k):
        src, _, r0, rows, c0 = chunks[k]
        return pltpu.make_async_copy(
            src.at[r0:r0 + rows, c0:c0 + stage_cols], stage_ref.at[k % 2, 0:rows, :], sem_ref.at[k % 2])

    copy(0).start()
    for k, (_, dst, r0, rows, c0) in enumerate(chunks):
        if k + 1 < len(chunks):
            copy(k + 1).start()
        copy(k).wait()
        dst[r0:r0 + rows, c0:c0 + stage_cols] = stage_ref[k % 2, 0:rows, :].astype(BF16)


def _post_kernel(x_ref, ypool_ref, ysb_ref, p_ref, wout_hbm, gpost_ref, gpre_ref, wup_hbm, wdown_hbm,
                 gmlp_ref, wgate_hbm, wple_hbm, gple_ref, o_ref,
                 wout_ref, wup_ref, wdown_ref, wgate_ref, wple_ref, stage_ref, sem_ref):
    @pl.when(pl.program_id(0) == 0)
    def _():
        _load_weights_as_bf16(
            [(wout_hbm, wout_ref), (wup_hbm, wup_ref), (wdown_hbm, wdown_ref), (wgate_hbm, wgate_ref),
             (wple_hbm, wple_ref)], stage_ref, sem_ref)

    pw = ypool_ref.shape[1]
    d_model = wup_ref.shape[0]
    n_sub = x_ref.shape[0] // SUB_TILE
    sub_rows = lambda k: slice(k * SUB_TILE, (k + 1) * SUB_TILE)
    h, hn, m = {}, {}, {}

    def norms(k, mix):
        h[k] = x_ref[sub_rows(k), :] + _rms_norm(mix, gpost_ref[...])
        hn[k] = _rms_norm(h[k], gpre_ref[...]).astype(BF16)

    def mlp(k, after_first_up):
        acc = None
        for c0 in range(0, wup_ref.shape[1], d_model):
            up = _dot(hn[k], wup_ref[:, c0:c0 + d_model])
            if c0 == 0:
                after_first_up()
            act = jnp.square(jnp.maximum(up, 0.0)).astype(BF16)
            part = _dot(act, wdown_ref[c0:c0 + d_model, :])
            acc = part if acc is None else acc + part
        m[k] = acc

    h2, gate_logits = {}, {}

    def post_norm(k):
        h2[k] = h.pop(k) + _rms_norm(m.pop(k), gmlp_ref[...])

    def gate_dot(k):
        gate_logits[k] = _dot(h2[k].astype(BF16), wgate_ref[...])

    def emit(k):
        o_ref[sub_rows(k), :] = h2.pop(k) + jax.nn.sigmoid(gate_logits.pop(k)) * e[k]

    def mix(k):
        return (_dot(ypool_ref[sub_rows(k), :], wout_ref[0:pw, :])
                + _dot(ysb_ref[sub_rows(k), :], wout_ref[pw:, :]))

    norms(0, mix(0))
    mixes = [None] + [mix(k) for k in range(1, n_sub)]
    e = [_rms_norm(_dot(p_ref[sub_rows(k), :].astype(BF16), wple_ref[...]), gple_ref[...])
         for k in range(n_sub)]
    for k in range(n_sub):
        def between(k=k):
            if k + 1 < n_sub:
                norms(k + 1, mixes[k + 1])
            if k >= 1:
                post_norm(k - 1)
            if k >= 2:
                gate_dot(k - 2)
                emit(k - 2)
        mlp(k, between)
    if n_sub >= 2:
        gate_dot(n_sub - 2)
    post_norm(n_sub - 1)
    gate_dot(n_sub - 1)
    if n_sub >= 2:
        emit(n_sub - 2)
    emit(n_sub - 1)


def _post_mlp(x, ypool, ysb, p, wout, gpost, gpre, wup, wdown, gmlp, wgate, wple, gple):
    t, d = x.shape
    row_blk = lambda width: pl.BlockSpec((POST_ROW_TILE, width), lambda i: (i, 0))
    resident = lambda a: pl.BlockSpec(a.shape, lambda i: (0, 0), pipeline_mode=pl.Buffered(1))
    in_hbm = pl.BlockSpec(memory_space=pl.ANY)
    weights = (wout, wup, wdown, wgate, wple)
    assert all(w.shape[1] % d == 0 for w in weights)
    return pl.pallas_call(
        _post_kernel,
        grid=(t // POST_ROW_TILE,),
        in_specs=[
            row_blk(d), row_blk(ypool.shape[1]), row_blk(ysb.shape[1]), row_blk(p.shape[1]),
            in_hbm, resident(gpost), resident(gpre), in_hbm, in_hbm,
            resident(gmlp), in_hbm, in_hbm, resident(gple),
        ],
        out_specs=row_blk(d),
        out_shape=jax.ShapeDtypeStruct((t, d), F32),
        scratch_shapes=[pltpu.VMEM(w.shape, BF16) for w in weights] + [
            pltpu.VMEM((2, WEIGHT_STAGE_ROWS, d), F32),
            pltpu.SemaphoreType.DMA((2,)),
        ],
        compiler_params=pltpu.CompilerParams(
            dimension_semantics=("arbitrary",),
            vmem_limit_bytes=60 * 1024 * 1024),
        name="post_mlp",
    )(x, ypool, ysb, p, wout, gpost, gpre, wup, wdown, gmlp, wgate, wple, gple)


@jax.jit
def kernel(x, p, g_mix_pre, w_in, w_pool, pool_scale, g_sb, w_out, g_mix_post, g_mlp_pre, w_up,
           w_down, g_mlp_post, w_ple_gate, w_ple_proj, g_ple):
    b, s, d = x.shape
    depth = w_in.shape[0]
    pw = pool_scale.shape[1]
    sw = g_sb.shape[1]
    row = lambda a: a.reshape(1, -1)
    h = x
    for i in range(depth):
        w = w_in[i].astype(BF16)
        wu, wq, wk, wv = w[:, :pw], w[:, pw:pw + sw], w[:, pw + sw:pw + 2 * sw], w[:, pw + 2 * sw:]
        ypool, q, kt, v = _inproj_pool(h, row(g_mix_pre[i]), wu, wq, wk.T, wv,
                                       w_pool[i].astype(BF16), row(pool_scale[i]))
        ysb = _sb_attention(q, kt, v, row(g_sb[i]))
        t = b * s
        h = _post_mlp(
            h.reshape(t, d), ypool.reshape(t, pw), ysb.reshape(t, sw), p[i].reshape(t, -1),
            w_out[i], row(g_mix_post[i]), row(g_mlp_pre[i]), w_up[i], w_down[i],
            row(g_mlp_post[i]), w_ple_gate[i], w_ple_proj[i], row(g_ple[i])).reshape(b, s, d)
    return h
```

```python
import jax
import jax.numpy as jnp
from jax import lax
from jax.experimental import pallas as pl
from jax.experimental.pallas import tpu as pltpu

F32 = jnp.float32
BF16 = jnp.bfloat16

EPS = 1e-6
POOL_WINDOWS = (2, 4, 8, 16)
HEAD_DIM = 64
Q_SCALE = HEAD_DIM ** -0.5
LOG2E = 1.4426950408889634

LANES = 128
HEADS_PER_TILE = LANES // HEAD_DIM

ROW_TILE = 1024
POST_ROW_TILE = 512
SUB_TILE = 256
WEIGHT_STAGE_ROWS = 256
WEIGHT_STAGE_SLOTS = 4
POOL_HALO = 16
TQ = 256
TK = 256
ROW_CHUNK = 128
PAIRS_PER_STEP = 4
PIPE_LAGS = (3, 3)
STICK_EXHAUSTED = 104.0
MASK_BIAS = -1e30

_NT_DIMS = (((1,), (1,)), ((), ()))


def _dot(a, b):
    return jnp.dot(a, b, preferred_element_type=F32)


def _rms_norm(x, g):
    ms = jnp.mean(x * x, axis=-1, keepdims=True)
    return x * lax.rsqrt(ms + EPS) * g


def _inproj_pool_kernel(x_ref, g_ref, wu_ref, wq_ref, wkt_ref, wv_ref, wpool_ref, pscale_ref,
                        ypool_ref, q_ref, kt_ref, v_ref, uext_ref):
    s = pl.program_id(1)
    rows = x_ref.shape[0]
    n_sub = rows // SUB_TILE
    sub_rows = lambda k: slice(k * SUB_TILE, (k + 1) * SUB_TILE)
    n_groups = len(POOL_WINDOWS)

    @pl.when(s == 0)
    def _():
        uext_ref[0:POOL_HALO, :] = jnp.zeros((POOL_HALO, uext_ref.shape[1]), F32)

    def normed(k):
        return _rms_norm(x_ref[sub_rows(k), :], g_ref[...]).astype(BF16)

    def project_u(k, hn):
        uext_ref[POOL_HALO + k * SUB_TILE:POOL_HALO + (k + 1) * SUB_TILE, :] = _dot(hn, wu_ref[...])

    def project_q(k, hn):
        q_ref[sub_rows(k), :] = (_dot(hn, wq_ref[...]) * Q_SCALE).astype(BF16)

    def project_v(k, hn):
        v_ref[sub_rows(k), :] = _dot(hn, wv_ref[...]).astype(BF16)

    def project_kt(k, hn):
        kt = lax.dot_general(wkt_ref[...], hn, _NT_DIMS, preferred_element_type=F32).astype(BF16)
        for hp in range(kt_ref.shape[0]):
            for kb in range(SUB_TILE // TK):
                kt_ref[hp, k * (SUB_TILE // TK) + kb] = kt[hp * LANES:(hp + 1) * LANES, kb * TK:(kb + 1) * TK]

    def pool(k, groups):
        base = POOL_HALO + k * SUB_TILE
        t = s * rows + k * SUB_TILE + lax.broadcasted_iota(jnp.int32, (SUB_TILE, 1), 0)
        for gi in groups:
            w = POOL_WINDOWS[gi]
            cols = slice(gi * LANES, (gi + 1) * LANES)
            ug = uext_ref[base:base + SUB_TILE, cols]
            acc = ug
            for i in range(1, w):
                acc = acc + uext_ref[base - i:base - i + SUB_TILE, cols]
            cnt = jnp.minimum(t + 1, w).astype(F32)
            d = acc / cnt - ug
            y = _dot(d.astype(BF16), wpool_ref[gi]) * pscale_ref[:, cols]
            ypool_ref[sub_rows(k), cols] = y.astype(BF16)

    hn = {0: normed(0)}
    project_u(0, hn[0])
    for k in range(n_sub):
        if k + 1 < n_sub:
            hn[k + 1] = normed(k + 1)
        project_q(k, hn[k])
        pool(k, range(0, n_groups // 2))
        if k + 1 < n_sub:
            project_u(k + 1, hn[k + 1])
        project_v(k, hn[k])
        pool(k, range(n_groups // 2, n_groups))
        project_kt(k, hn.pop(k))
    uext_ref[0:POOL_HALO, :] = uext_ref[rows:rows + POOL_HALO, :]


def _inproj_pool(x, g, wu, wq, wkt, wv, wpool, pscale):
    b, s, d = x.shape
    pw = wu.shape[1]
    sw = wq.shape[1]
    n_pairs = sw // LANES
    kb_per_tile = ROW_TILE // TK
    const2 = lambda bi, si: (0, 0)
    row_blk = lambda width: pl.BlockSpec((None, ROW_TILE, width), lambda bi, si: (bi, si, 0))
    return pl.pallas_call(
        _inproj_pool_kernel,
        grid=(b, s // ROW_TILE),
        in_specs=[
            row_blk(d),
            pl.BlockSpec((1, d), const2),
            pl.BlockSpec((d, pw), const2),
            pl.BlockSpec((d, sw), const2),
            pl.BlockSpec((sw, d), const2),
            pl.BlockSpec((d, sw), const2),
            pl.BlockSpec(wpool.shape, lambda bi, si: (0, 0, 0)),
            pl.BlockSpec((1, pw), const2),
        ],
        out_specs=[
            row_blk(pw),
            row_blk(sw),
            pl.BlockSpec((None, n_pairs, kb_per_tile, LANES, TK), lambda bi, si: (bi, 0, si, 0, 0)),
            row_blk(sw),
        ],
        out_shape=[
            jax.ShapeDtypeStruct((b, s, pw), BF16),
            jax.ShapeDtypeStruct((b, s, sw), BF16),
            jax.ShapeDtypeStruct((b, n_pairs, s // TK, LANES, TK), BF16),
            jax.ShapeDtypeStruct((b, s, sw), BF16),
        ],
        scratch_shapes=[pltpu.VMEM((ROW_TILE + POOL_HALO, pw), F32)],
        compiler_params=pltpu.CompilerParams(
            dimension_semantics=("arbitrary", "arbitrary"),
            vmem_limit_bytes=40 * 1024 * 1024),
        name="inproj_pool",
    )(x, g, wu, wq, wkt, wv, wpool, pscale)


def _softplus(z):
    return jnp.maximum(z, 0.0) + jnp.log(1.0 + jnp.exp2(jnp.abs(z) * (-LOG2E)))


def _cumulative_softplus(z, tri):
    return _dot(_softplus(z).astype(BF16), tri)


def _head_rms_norm(acc_first, acc_second, gain, first_head):
    o = jnp.where(first_head, acc_first, acc_second)
    o2 = o * o
    ms0 = jnp.sum(jnp.where(first_head, o2, 0.0), axis=-1, keepdims=True) * (1.0 / HEAD_DIM)
    ms1 = jnp.sum(jnp.where(first_head, 0.0, o2), axis=-1, keepdims=True) * (1.0 / HEAD_DIM)
    inv = jnp.where(first_head, lax.rsqrt(ms0 + EPS), lax.rsqrt(ms1 + EPS))
    return (o * inv * gain).astype(BF16)


def _attn_kernel(q_ref, kt_ref, v_ref, g_ref, tri_hbm, bias_hbm, o_ref,
                 tri_ref, bias_ref, const_sem, acc_ref, carry_ref):
    i = pl.program_id(2)
    n_pairs = kt_ref.shape[0]

    @pl.when((pl.program_id(0) == 0) & (pl.program_id(1) == 0) & (i == 0))
    def _():
        copies = (pltpu.make_async_copy(tri_hbm, tri_ref, const_sem.at[0]),
                  pltpu.make_async_copy(bias_hbm, bias_ref, const_sem.at[1]))
        for c in copies:
            c.start()
        for c in copies:
            c.wait()

    def tri(nk):
        return tri_ref[0:nk, 0:nk]

    first_head = lax.broadcasted_iota(jnp.int32, (ROW_CHUNK, LANES), 1) < HEAD_DIM
    pair_lanes = lambda p: slice(p * LANES, (p + 1) * LANES)

    def q_head(p, h, rows):
        qc = q_ref[rows, pair_lanes(p)]
        zero = jnp.zeros_like(qc)
        return jnp.where(first_head, qc, zero) if h == 0 else jnp.where(first_head, zero, qc)

    j_prev = jnp.maximum(i - 1, 0)
    no_prev = jnp.where(i >= 1, 0.0, -MASK_BIAS)
    units = [(p, r0, h, blk) for p in range(n_pairs) for r0 in range(0, TQ, ROW_CHUNK)
             for h in range(HEADS_PER_TILE) for blk in ("diag", "prev")]
    qm, z, cin, part, carry, acc = {}, {}, {}, {}, {}, {}

    def n_keys(u):
        return u[1] + ROW_CHUNK if u[3] == "diag" else TK - u[1]

    def scores(u):
        p, r0, h, blk = u
        rows = slice(r0, r0 + ROW_CHUNK)
        if blk == "diag":
            qm[p, r0, h] = q_head(p, h, rows)
            z[u] = _dot(qm[p, r0, h], kt_ref[p, i, :, 0:n_keys(u)]) + bias_ref[rows, 0:n_keys(u)]
        else:
            z[u] = _dot(qm[p, r0, h], kt_ref[p, j_prev, :, r0:TK])

    def stick(u):
        cin[u] = _cumulative_softplus(z[u], tri(n_keys(u)))

    def weights_times_values(u):
        p, r0, h, blk = u
        rows = slice(r0, r0 + ROW_CHUNK)
        if blk == "diag":
            a = jnp.exp(z.pop(u) - cin[u]).astype(BF16)
            part[p, r0, h] = _dot(a, v_ref[i, 0:n_keys(u), pair_lanes(p)])
            carry[p, r0, h] = cin.pop(u)[:, 0:1]
            return
        c_in = carry.pop((p, r0, h))
        a = jnp.exp(z.pop(u) - cin[u] - (c_in + no_prev)).astype(BF16)
        acc[p, r0, h] = part.pop((p, r0, h)) + _dot(a, v_ref[j_prev, r0:TK, pair_lanes(p)])
        acc_ref[p, h, rows] = acc[p, r0, h]
        carry_ref[p, h, rows] = c_in + cin.pop(u)[:, 0:1]
        if h == HEADS_PER_TILE - 1:
            o_ref[rows, pair_lanes(p)] = _head_rms_norm(
                acc.pop((p, r0, 0)), acc.pop((p, r0, 1)), g_ref[:, pair_lanes(p)], first_head)

    stages = ((scores, 0), (stick, PIPE_LAGS[0]), (weights_times_values, PIPE_LAGS[0] + PIPE_LAGS[1]))
    for t in range(len(units) + stages[-1][1]):
        for stage, delay in stages:
            if 0 <= t - delay < len(units):
                stage(units[t - delay])

    @pl.when(jnp.logical_and(i >= 1, jnp.min(carry_ref[...]) <= STICK_EXHAUSTED))
    def _():
        for p in range(n_pairs):
            for h in range(HEADS_PER_TILE):
                for r0 in range(ROW_CHUNK, TQ, ROW_CHUNK):
                    rows = slice(r0, r0 + ROW_CHUNK)
                    c_in = carry_ref[p, h, rows]
                    zj = _dot(q_head(p, h, rows), kt_ref[p, j_prev, :, 0:r0])
                    cj = _cumulative_softplus(zj, tri(r0))
                    a = jnp.exp(zj - cj - c_in).astype(BF16)
                    acc_ref[p, h, rows] += _dot(a, v_ref[j_prev, 0:r0, pair_lanes(p)])
                    carry_ref[p, h, rows] = c_in + cj[:, 0:1]

        def unfinished(state):
            j, min_carry = state
            return jnp.logical_and(j >= 0, min_carry <= STICK_EXHAUSTED)

        def step(state):
            j, _ = state
            for p in range(n_pairs):
                for h in range(HEADS_PER_TILE):
                    c_in = carry_ref[p, h]
                    qh = jnp.concatenate(
                        [q_head(p, h, slice(r0, r0 + ROW_CHUNK)) for r0 in range(0, TQ, ROW_CHUNK)], axis=0)
                    zj = _dot(qh, kt_ref[p, j])
                    cj = _cumulative_softplus(zj, tri(TK))
                    a = jnp.exp(zj - cj - c_in).astype(BF16)
                    acc_ref[p, h] += _dot(a, v_ref[j, :, pair_lanes(p)])
                    carry_ref[p, h] = c_in + cj[:, 0:1]
            return j - 1, jnp.min(carry_ref[...])

        lax.while_loop(unfinished, step, (i - 2, jnp.min(carry_ref[...])))
        for p in range(n_pairs):
            for r0 in range(0, TQ, ROW_CHUNK):
                rows = slice(r0, r0 + ROW_CHUNK)
                o_ref[rows, pair_lanes(p)] = _head_rms_norm(
                    acc_ref[p, 0, rows], acc_ref[p, 1, rows], g_ref[:, pair_lanes(p)], first_head)


def _sb_attention(q, kt, v, g_sb):
    b, s, sw = q.shape
    group_lanes = PAIRS_PER_STEP * LANES
    n_groups = sw // group_lanes
    n_kb = s // TK
    v_blocks = v.reshape(b, n_kb, TK, sw)
    r = lax.broadcasted_iota(jnp.int32, (TK, TK), 0)
    c = lax.broadcasted_iota(jnp.int32, (TK, TK), 1)
    tri = jnp.where(r >= c, 1.0, 0.0).astype(BF16)
    bias = jnp.where(c < r, 0.0, MASK_BIAS).astype(F32)
    const = lambda bi, gi, i: (0, 0)
    return pl.pallas_call(
        _attn_kernel,
        grid=(b, n_groups, s // TQ),
        in_specs=[
            pl.BlockSpec((None, TQ, group_lanes), lambda bi, gi, i: (bi, i, gi)),
            pl.BlockSpec((None, PAIRS_PER_STEP, n_kb, LANES, TK), lambda bi, gi, i: (bi, gi, 0, 0, 0)),
            pl.BlockSpec((None, n_kb, TK, group_lanes), lambda bi, gi, i: (bi, 0, 0, gi)),
            pl.BlockSpec((1, group_lanes), lambda bi, gi, i: (0, gi)),
            pl.BlockSpec(memory_space=pl.ANY),
            pl.BlockSpec(memory_space=pl.ANY),
        ],
        out_specs=pl.BlockSpec((None, TQ, group_lanes), lambda bi, gi, i: (bi, i, gi)),
        out_shape=jax.ShapeDtypeStruct((b, s, sw), BF16),
        scratch_shapes=[
            pltpu.VMEM((TK, TK), BF16),
            pltpu.VMEM((TQ, TK), F32),
            pltpu.SemaphoreType.DMA((2,)),
            pltpu.VMEM((PAIRS_PER_STEP, HEADS_PER_TILE, TQ, LANES), F32),
            pltpu.VMEM((PAIRS_PER_STEP, HEADS_PER_TILE, TQ, 1), F32),
        ],
        compiler_params=pltpu.CompilerParams(
            dimension_semantics=("arbitrary", "arbitrary", "arbitrary"),
            vmem_limit_bytes=48 * 1024 * 1024),
        name="sb_attn",
    )(q, kt, v_blocks, g_sb, tri, bias)


def _load_weights_as_bf16(pairs, stage_ref, sem_ref):
    n_slots, stage_rows, stage_cols = stage_ref.shape
    chunks = []
    for src, dst in pairs:
        n_rows, n_cols = src.shape
        for r0 in range(0, n_rows, stage_rows):
            rows = min(stage_rows, n_rows - r0)
            for c0 in range(0, n_cols, stage_cols):
                chunks.append((src, dst, r0, rows, c0))

    def copy(k):
        src, _, r0, rows, c0 = chunks[k]
        slot = k % n_slots
        return pltpu.make_async_copy(
            src.at[r0:r0 + rows, c0:c0 + stage_cols], stage_ref.at[slot, 0:rows, :], sem_ref.at[slot])

    lookahead = n_slots - 1
    for k in range(min(lookahead, len(chunks))):
        copy(k).start()
    for k, (_, dst, r0, rows, c0) in enumerate(chunks):
        if k + lookahead < len(chunks):
            copy(k + lookahead).start()
        copy(k).wait()
        dst[r0:r0 + rows, c0:c0 + stage_cols] = stage_ref[k % n_slots, 0:rows, :].astype(BF16)


def _post_kernel(x_ref, ypool_ref, ysb_ref, p_ref, wout_hbm, gpost_ref, gpre_ref, wup_hbm, wdown_hbm,
                 gmlp_ref, wgate_hbm, wple_hbm, gple_ref, o_ref,
                 wout_ref, wup_ref, wdown_ref, wgate_ref, wple_ref, stage_ref, sem_ref):
    @pl.when(pl.program_id(0) == 0)
    def _():
        _load_weights_as_bf16(
            [(wout_hbm, wout_ref), (wup_hbm, wup_ref), (wdown_hbm, wdown_ref), (wgate_hbm, wgate_ref),
             (wple_hbm, wple_ref)], stage_ref, sem_ref)

    pw = ypool_ref.shape[1]
    d_model = wup_ref.shape[0]
    n_sub = x_ref.shape[0] // SUB_TILE
    sub_rows = lambda k: slice(k * SUB_TILE, (k + 1) * SUB_TILE)
    h, hn, m = {}, {}, {}

    def norms(k, mix):
        h[k] = x_ref[sub_rows(k), :] + _rms_norm(mix, gpost_ref[...])
        hn[k] = _rms_norm(h[k], gpre_ref[...]).astype(BF16)

    def mlp(k, after_first_up):
        acc = None
        for c0 in range(0, wup_ref.shape[1], d_model):
            up = _dot(hn[k], wup_ref[:, c0:c0 + d_model])
            if c0 == 0:
                after_first_up()
            act = jnp.square(jnp.maximum(up, 0.0)).astype(BF16)
            part = _dot(act, wdown_ref[c0:c0 + d_model, :])
            acc = part if acc is None else acc + part
        m[k] = acc

    h2, gate_logits = {}, {}

    def post_norm(k):
        h2[k] = h.pop(k) + _rms_norm(m.pop(k), gmlp_ref[...])

    def gate_dot(k):
        gate_logits[k] = _dot(h2[k].astype(BF16), wgate_ref[...])

    def emit(k):
        o_ref[sub_rows(k), :] = h2.pop(k) + jax.nn.sigmoid(gate_logits.pop(k)) * e[k]

    def mix(k):
        return (_dot(ypool_ref[sub_rows(k), :], wout_ref[0:pw, :])
                + _dot(ysb_ref[sub_rows(k), :], wout_ref[pw:, :]))

    norms(0, mix(0))
    mixes = [None] + [mix(k) for k in range(1, n_sub)]
    e = [_rms_norm(_dot(p_ref[sub_rows(k), :].astype(BF16), wple_ref[...]), gple_ref[...])
         for k in range(n_sub)]
    for k in range(n_sub):
        def between(k=k):
            if k + 1 < n_sub:
                norms(k + 1, mixes[k + 1])
            if k >= 1:
                post_norm(k - 1)
            if k >= 2:
                gate_dot(k - 2)
                emit(k - 2)
        mlp(k, between)
    if n_sub >= 2:
        gate_dot(n_sub - 2)
    post_norm(n_sub - 1)
    gate_dot(n_sub - 1)
    if n_sub >= 2:
        emit(n_sub - 2)
    emit(n_sub - 1)


def _post_mlp(x, ypool, ysb, p, wout, gpost, gpre, wup, wdown, gmlp, wgate, wple, gple):
    t, d = x.shape
    row_blk = lambda width: pl.BlockSpec((POST_ROW_TILE, width), lambda i: (i, 0))
    resident = lambda a: pl.BlockSpec(a.shape, lambda i: (0, 0), pipeline_mode=pl.Buffered(1))
    in_hbm = pl.BlockSpec(memory_space=pl.ANY)
    weights = (wout, wup, wdown, wgate, wple)
    assert all(w.shape[1] % d == 0 for w in weights)
    return pl.pallas_call(
        _post_kernel,
        grid=(t // POST_ROW_TILE,),
        in_specs=[
            row_blk(d), row_blk(ypool.shape[1]), row_blk(ysb.shape[1]), row_blk(p.shape[1]),
            in_hbm, resident(gpost), resident(gpre), in_hbm, in_hbm,
            resident(gmlp), in_hbm, in_hbm, resident(gple),
        ],
        out_specs=row_blk(d),
        out_shape=jax.ShapeDtypeStruct((t, d), F32),
        scratch_shapes=[pltpu.VMEM(w.shape, BF16) for w in weights] + [
            pltpu.VMEM((WEIGHT_STAGE_SLOTS, WEIGHT_STAGE_ROWS, d), F32),
            pltpu.SemaphoreType.DMA((WEIGHT_STAGE_SLOTS,)),
        ],
        compiler_params=pltpu.CompilerParams(
            dimension_semantics=("arbitrary",),
            vmem_limit_bytes=60 * 1024 * 1024),
        name="post_mlp",
    )(x, ypool, ysb, p, wout, gpost, gpre, wup, wdown, gmlp, wgate, wple, gple)


@jax.jit
def kernel(x, p, g_mix_pre, w_in, w_pool, pool_scale, g_sb, w_out, g_mix_post, g_mlp_pre, w_up,
           w_down, g_mlp_post, w_ple_gate, w_ple_proj, g_ple):
    b, s, d = x.shape
    depth = w_in.shape[0]
    pw = pool_scale.shape[1]
    sw = g_sb.shape[1]
    row = lambda a: a.reshape(1, -1)
    h = x
    for i in range(depth):
        w = w_in[i].astype(BF16)
        wu, wq, wk, wv = w[:, :pw], w[:, pw:pw + sw], w[:, pw + sw:pw + 2 * sw], w[:, pw + 2 * sw:]
        ypool, q, kt, v = _inproj_pool(h, row(g_mix_pre[i]), wu, wq, wk.T, wv,
                                       w_pool[i].astype(BF16), row(pool_scale[i]))
        ysb = _sb_attention(q, kt, v, row(g_sb[i]))
        t = b * s
        h = _post_mlp(
            h.reshape(t, d), ypool.reshape(t, pw), ysb.reshape(t, sw), p[i].reshape(t, -1),
            w_out[i], row(g_mix_post[i]), row(g_mlp_pre[i]), w_up[i], w_down[i],
            row(g_mlp_post[i]), w_ple_gate[i], w_ple_proj[i], row(g_ple[i])).reshape(b, s, d)
    return h
```

```python
import jax
import jax.numpy as jnp
from jax import lax
from jax.experimental import pallas as pl
from jax.experimental.pallas import tpu as pltpu

F32 = jnp.float32
BF16 = jnp.bfloat16

EPS = 1e-6
POOL_WINDOWS = (2, 4, 8, 16)
HEAD_DIM = 64
Q_SCALE = HEAD_DIM ** -0.5
LOG2E = 1.4426950408889634

LANES = 128
HEADS_PER_TILE = LANES // HEAD_DIM

ROW_TILE = 1024
POST_ROW_TILE = 512
SUB_TILE = 256
WEIGHT_STAGE_ROWS = 256
WEIGHT_STAGE_SLOTS = 4
POOL_HALO = 16
TQ = 256
TK = 256
ROW_CHUNK = 128
PAIRS_PER_STEP = 4
PIPE_LAGS = (3, 3)
STICK_EXHAUSTED = 104.0
MASK_BIAS = -1e30

_NT_DIMS = (((1,), (1,)), ((), ()))


def _dot(a, b):
    return jnp.dot(a, b, preferred_element_type=F32)


def _rms_norm(x, g):
    ms = jnp.mean(x * x, axis=-1, keepdims=True)
    return x * lax.rsqrt(ms + EPS) * g


def _inproj_pool_kernel(x_ref, g_ref, wu_ref, wq_ref, wkt_ref, wv_ref, wpool_ref, pscale_ref,
                        ypool_ref, q_ref, kt_ref, v_ref, uext_ref):
    s = pl.program_id(1)
    rows = x_ref.shape[0]
    n_sub = rows // SUB_TILE
    sub_rows = lambda k: slice(k * SUB_TILE, (k + 1) * SUB_TILE)
    n_groups = len(POOL_WINDOWS)

    @pl.when(s == 0)
    def _():
        uext_ref[0:POOL_HALO, :] = jnp.zeros((POOL_HALO, uext_ref.shape[1]), F32)

    def normed(k):
        return _rms_norm(x_ref[sub_rows(k), :], g_ref[...]).astype(BF16)

    def project_u(k, hn):
        uext_ref[POOL_HALO + k * SUB_TILE:POOL_HALO + (k + 1) * SUB_TILE, :] = _dot(hn, wu_ref[...])

    def project_q(k, hn):
        q_ref[sub_rows(k), :] = (_dot(hn, wq_ref[...]) * Q_SCALE).astype(BF16)

    def project_v(k, hn):
        v_ref[sub_rows(k), :] = _dot(hn, wv_ref[...]).astype(BF16)

    def project_kt(k, hn):
        kt = lax.dot_general(wkt_ref[...], hn, _NT_DIMS, preferred_element_type=F32).astype(BF16)
        for hp in range(kt_ref.shape[0]):
            for kb in range(SUB_TILE // TK):
                kt_ref[hp, k * (SUB_TILE // TK) + kb] = kt[hp * LANES:(hp + 1) * LANES, kb * TK:(kb + 1) * TK]

    def pool(k, groups):
        base = POOL_HALO + k * SUB_TILE
        t = s * rows + k * SUB_TILE + lax.broadcasted_iota(jnp.int32, (SUB_TILE, 1), 0)
        for gi in groups:
            w = POOL_WINDOWS[gi]
            cols = slice(gi * LANES, (gi + 1) * LANES)
            ug = uext_ref[base:base + SUB_TILE, cols]
            acc = ug
            for i in range(1, w):
                acc = acc + uext_ref[base - i:base - i + SUB_TILE, cols]
            cnt = jnp.minimum(t + 1, w).astype(F32)
            d = acc / cnt - ug
            y = _dot(d.astype(BF16), wpool_ref[gi]) * pscale_ref[:, cols]
            ypool_ref[sub_rows(k), cols] = y.astype(BF16)

    hn = {0: normed(0)}
    project_u(0, hn[0])
    for k in range(n_sub):
        if k + 1 < n_sub:
            hn[k + 1] = normed(k + 1)
        project_q(k, hn[k])
        pool(k, range(0, n_groups // 2))
        if k + 1 < n_sub:
            project_u(k + 1, hn[k + 1])
        project_v(k, hn[k])
        pool(k, range(n_groups // 2, n_groups))
        project_kt(k, hn.pop(k))
    uext_ref[0:POOL_HALO, :] = uext_ref[rows:rows + POOL_HALO, :]


def _inproj_pool(x, g, wu, wq, wkt, wv, wpool, pscale):
    b, s, d = x.shape
    pw = wu.shape[1]
    sw = wq.shape[1]
    n_pairs = sw // LANES
    assert s % ROW_TILE == 0 and ROW_TILE % SUB_TILE == 0 and SUB_TILE % TK == 0
    assert pw == len(POOL_WINDOWS) * LANES and sw % LANES == 0 and max(POOL_WINDOWS) <= POOL_HALO
    kb_per_tile = ROW_TILE // TK
    const2 = lambda bi, si: (0, 0)
    row_blk = lambda width: pl.BlockSpec((None, ROW_TILE, width), lambda bi, si: (bi, si, 0))
    return pl.pallas_call(
        _inproj_pool_kernel,
        grid=(b, s // ROW_TILE),
        in_specs=[
            row_blk(d),
            pl.BlockSpec((1, d), const2),
            pl.BlockSpec((d, pw), const2),
            pl.BlockSpec((d, sw), const2),
            pl.BlockSpec((sw, d), const2),
            pl.BlockSpec((d, sw), const2),
            pl.BlockSpec(wpool.shape, lambda bi, si: (0, 0, 0)),
            pl.BlockSpec((1, pw), const2),
        ],
        out_specs=[
            row_blk(pw),
            row_blk(sw),
            pl.BlockSpec((None, n_pairs, kb_per_tile, LANES, TK), lambda bi, si: (bi, 0, si, 0, 0)),
            row_blk(sw),
        ],
        out_shape=[
            jax.ShapeDtypeStruct((b, s, pw), BF16),
            jax.ShapeDtypeStruct((b, s, sw), BF16),
            jax.ShapeDtypeStruct((b, n_pairs, s // TK, LANES, TK), BF16),
            jax.ShapeDtypeStruct((b, s, sw), BF16),
        ],
        scratch_shapes=[pltpu.VMEM((ROW_TILE + POOL_HALO, pw), F32)],
        compiler_params=pltpu.CompilerParams(
            dimension_semantics=("arbitrary", "arbitrary"),
            vmem_limit_bytes=40 * 1024 * 1024),
        name="inproj_pool",
    )(x, g, wu, wq, wkt, wv, wpool, pscale)


def _softplus(z):
    return jnp.maximum(z, 0.0) + jnp.log(1.0 + jnp.exp2(jnp.abs(z) * (-LOG2E)))


def _cumulative_softplus(z, tri):
    return _dot(_softplus(z).astype(BF16), tri)


def _head_rms_norm(acc_first, acc_second, gain, first_head):
    o = jnp.where(first_head, acc_first, acc_second)
    o2 = o * o
    ms0 = jnp.sum(jnp.where(first_head, o2, 0.0), axis=-1, keepdims=True) * (1.0 / HEAD_DIM)
    ms1 = jnp.sum(jnp.where(first_head, 0.0, o2), axis=-1, keepdims=True) * (1.0 / HEAD_DIM)
    inv = jnp.where(first_head, lax.rsqrt(ms0 + EPS), lax.rsqrt(ms1 + EPS))
    return (o * inv * gain).astype(BF16)


def _attn_kernel(q_ref, kt_ref, v_ref, g_hbm, tri_hbm, bias_hbm, o_ref,
                 g_ref, tri_ref, bias_ref, const_sem, acc_ref, carry_ref):
    i = pl.program_id(2)
    n_pairs = kt_ref.shape[0]

    @pl.when((pl.program_id(0) == 0) & (pl.program_id(1) == 0) & (i == 0))
    def _():
        copies = (pltpu.make_async_copy(tri_hbm, tri_ref, const_sem.at[0]),
                  pltpu.make_async_copy(bias_hbm, bias_ref, const_sem.at[1]),
                  pltpu.make_async_copy(g_hbm, g_ref, const_sem.at[2]))
        for c in copies:
            c.start()
        for c in copies:
            c.wait()

    def tri(nk):
        return tri_ref[0:nk, 0:nk]

    first_head = lax.broadcasted_iota(jnp.int32, (ROW_CHUNK, LANES), 1) < HEAD_DIM
    pair_lanes = lambda p: slice(p * LANES, (p + 1) * LANES)

    def q_head(p, h, rows):
        qc = q_ref[rows, pair_lanes(p)]
        zero = jnp.zeros_like(qc)
        return jnp.where(first_head, qc, zero) if h == 0 else jnp.where(first_head, zero, qc)

    j_prev = jnp.maximum(i - 1, 0)
    no_prev = jnp.where(i >= 1, 0.0, -MASK_BIAS)
    units = [(p, r0, h, blk) for p in range(n_pairs) for r0 in range(0, TQ, ROW_CHUNK)
             for h in range(HEADS_PER_TILE) for blk in ("diag", "prev")]
    qm, z, cin, part, carry, acc = {}, {}, {}, {}, {}, {}

    def n_keys(u):
        return u[1] + ROW_CHUNK if u[3] == "diag" else TK - u[1]

    def scores(u):
        p, r0, h, blk = u
        rows = slice(r0, r0 + ROW_CHUNK)
        if blk == "diag":
            qm[p, r0, h] = q_head(p, h, rows)
            z[u] = _dot(qm[p, r0, h], kt_ref[p, i, :, 0:n_keys(u)]) + bias_ref[rows, 0:n_keys(u)]
        else:
            z[u] = _dot(qm[p, r0, h], kt_ref[p, j_prev, :, r0:TK])

    def stick(u):
        cin[u] = _cumulative_softplus(z[u], tri(n_keys(u)))

    def weights_times_values(u):
        p, r0, h, blk = u
        rows = slice(r0, r0 + ROW_CHUNK)
        if blk == "diag":
            a = jnp.exp(z.pop(u) - cin[u]).astype(BF16)
            part[p, r0, h] = _dot(a, v_ref[i, 0:n_keys(u), pair_lanes(p)])
            carry[p, r0, h] = cin.pop(u)[:, 0:1]
            return
        c_in = carry.pop((p, r0, h))
        a = jnp.exp(z.pop(u) - cin[u] - (c_in + no_prev)).astype(BF16)
        acc[p, r0, h] = part.pop((p, r0, h)) + _dot(a, v_ref[j_prev, r0:TK, pair_lanes(p)])
        acc_ref[p, h, rows] = acc[p, r0, h]
        carry_ref[p, h, rows] = c_in + cin.pop(u)[:, 0:1]
        if h == HEADS_PER_TILE - 1:
            o_ref[rows, pair_lanes(p)] = _head_rms_norm(
                acc.pop((p, r0, 0)), acc.pop((p, r0, 1)), g_ref[:, pair_lanes(p)], first_head)

    stages = ((scores, 0), (stick, PIPE_LAGS[0]), (weights_times_values, PIPE_LAGS[0] + PIPE_LAGS[1]))
    for t in range(len(units) + stages[-1][1]):
        for stage, delay in stages:
            if 0 <= t - delay < len(units):
                stage(units[t - delay])

    @pl.when(jnp.logical_and(i >= 1, jnp.min(carry_ref[...]) <= STICK_EXHAUSTED))
    def _():
        for p in range(n_pairs):
            for h in range(HEADS_PER_TILE):
                for r0 in range(ROW_CHUNK, TQ, ROW_CHUNK):
                    rows = slice(r0, r0 + ROW_CHUNK)
                    c_in = carry_ref[p, h, rows]
                    zj = _dot(q_head(p, h, rows), kt_ref[p, j_prev, :, 0:r0])
                    cj = _cumulative_softplus(zj, tri(r0))
                    a = jnp.exp(zj - cj - c_in).astype(BF16)
                    acc_ref[p, h, rows] += _dot(a, v_ref[j_prev, 0:r0, pair_lanes(p)])
                    carry_ref[p, h, rows] = c_in + cj[:, 0:1]

        def unfinished(state):
            j, min_carry = state
            return jnp.logical_and(j >= 0, min_carry <= STICK_EXHAUSTED)

        def step(state):
            j, _ = state
            for p in range(n_pairs):
                for h in range(HEADS_PER_TILE):
                    c_in = carry_ref[p, h]
                    qh = jnp.concatenate(
                        [q_head(p, h, slice(r0, r0 + ROW_CHUNK)) for r0 in range(0, TQ, ROW_CHUNK)], axis=0)
                    zj = _dot(qh, kt_ref[p, j])
                    cj = _cumulative_softplus(zj, tri(TK))
                    a = jnp.exp(zj - cj - c_in).astype(BF16)
                    acc_ref[p, h] += _dot(a, v_ref[j, :, pair_lanes(p)])
                    carry_ref[p, h] = c_in + cj[:, 0:1]
            return j - 1, jnp.min(carry_ref[...])

        lax.while_loop(unfinished, step, (i - 2, jnp.min(carry_ref[...])))
        for p in range(n_pairs):
            for r0 in range(0, TQ, ROW_CHUNK):
                rows = slice(r0, r0 + ROW_CHUNK)
                o_ref[rows, pair_lanes(p)] = _head_rms_norm(
                    acc_ref[p, 0, rows], acc_ref[p, 1, rows], g_ref[:, pair_lanes(p)], first_head)


def _sb_attention(q, kt, v, g_sb):
    b, s, sw = q.shape
    group_lanes = PAIRS_PER_STEP * LANES
    assert sw == group_lanes and s % TQ == 0 and TQ == TK and TQ % ROW_CHUNK == 0
    n_groups = sw // group_lanes
    n_kb = s // TK
    v_blocks = v.reshape(b, n_kb, TK, sw)
    r = lax.broadcasted_iota(jnp.int32, (TK, TK), 0)
    c = lax.broadcasted_iota(jnp.int32, (TK, TK), 1)
    tri = jnp.where(r >= c, 1.0, 0.0).astype(BF16)
    bias = jnp.where(c < r, 0.0, MASK_BIAS).astype(F32)
    return pl.pallas_call(
        _attn_kernel,
        grid=(b, n_groups, s // TQ),
        in_specs=[
            pl.BlockSpec((None, TQ, group_lanes), lambda bi, gi, i: (bi, i, gi)),
            pl.BlockSpec((None, PAIRS_PER_STEP, n_kb, LANES, TK), lambda bi, gi, i: (bi, gi, 0, 0, 0)),
            pl.BlockSpec((None, n_kb, TK, group_lanes), lambda bi, gi, i: (bi, 0, 0, gi)),
            pl.BlockSpec(memory_space=pl.ANY),
            pl.BlockSpec(memory_space=pl.ANY),
            pl.BlockSpec(memory_space=pl.ANY),
        ],
        out_specs=pl.BlockSpec((None, TQ, group_lanes), lambda bi, gi, i: (bi, i, gi)),
        out_shape=jax.ShapeDtypeStruct((b, s, sw), BF16),
        scratch_shapes=[
            pltpu.VMEM((1, sw), F32),
            pltpu.VMEM((TK, TK), BF16),
            pltpu.VMEM((TQ, TK), F32),
            pltpu.SemaphoreType.DMA((3,)),
            pltpu.VMEM((PAIRS_PER_STEP, HEADS_PER_TILE, TQ, LANES), F32),
            pltpu.VMEM((PAIRS_PER_STEP, HEADS_PER_TILE, TQ, 1), F32),
        ],
        compiler_params=pltpu.CompilerParams(
            dimension_semantics=("arbitrary", "arbitrary", "arbitrary"),
            vmem_limit_bytes=48 * 1024 * 1024),
        name="sb_attn",
    )(q, kt, v_blocks, g_sb, tri, bias)


def _load_weights_as_bf16(pairs, stage_ref, sem_ref):
    n_slots, stage_rows, stage_cols = stage_ref.shape
    chunks = []
    for src, dst in pairs:
        n_rows, n_cols = src.shape
        for r0 in range(0, n_rows, stage_rows):
            rows = min(stage_rows, n_rows - r0)
            for c0 in range(0, n_cols, stage_cols):
                chunks.append((src, dst, r0, rows, c0))

    def copy(k):
        src, _, r0, rows, c0 = chunks[k]
        slot = k % n_slots
        return pltpu.make_async_copy(
            src.at[r0:r0 + rows, c0:c0 + stage_cols], stage_ref.at[slot, 0:rows, :], sem_ref.at[slot])

    lookahead = n_slots - 1
    for k in range(min(lookahead, len(chunks))):
        copy(k).start()
    for k, (_, dst, r0, rows, c0) in enumerate(chunks):
        if k + lookahead < len(chunks):
            copy(k + lookahead).start()
        copy(k).wait()
        dst[r0:r0 + rows, c0:c0 + stage_cols] = stage_ref[k % n_slots, 0:rows, :].astype(BF16)


def _post_kernel(x_ref, ypool_ref, ysb_ref, p_ref, wout_hbm, gpost_ref, gpre_ref, wup_hbm, wdown_hbm,
                 gmlp_ref, wgate_hbm, wple_hbm, gple_ref, o_ref,
                 wout_ref, wup_ref, wdown_ref, wgate_ref, wple_ref, stage_ref, sem_ref):
    @pl.when(pl.program_id(0) == 0)
    def _():
        _load_weights_as_bf16(
            [(wout_hbm, wout_ref), (wup_hbm, wup_ref), (wdown_hbm, wdown_ref), (wgate_hbm, wgate_ref),
             (wple_hbm, wple_ref)], stage_ref, sem_ref)

    pw = ypool_ref.shape[1]
    d_model = wup_ref.shape[0]
    n_sub = x_ref.shape[0] // SUB_TILE
    sub_rows = lambda k: slice(k * SUB_TILE, (k + 1) * SUB_TILE)
    h, hn, m = {}, {}, {}

    def norms(k, mix):
        h[k] = x_ref[sub_rows(k), :] + _rms_norm(mix, gpost_ref[...])
        hn[k] = _rms_norm(h[k], gpre_ref[...]).astype(BF16)

    def mlp(k, after_first_up):
        acc = None
        for c0 in range(0, wup_ref.shape[1], d_model):
            up = _dot(hn[k], wup_ref[:, c0:c0 + d_model])
            if c0 == 0:
                after_first_up()
            act = jnp.square(jnp.maximum(up, 0.0)).astype(BF16)
            part = _dot(act, wdown_ref[c0:c0 + d_model, :])
            acc = part if acc is None else acc + part
        m[k] = acc

    h2, gate_logits = {}, {}

    def post_norm(k):
        h2[k] = h.pop(k) + _rms_norm(m.pop(k), gmlp_ref[...])

    def gate_dot(k):
        gate_logits[k] = _dot(h2[k].astype(BF16), wgate_ref[...])

    def emit(k):
        o_ref[sub_rows(k), :] = h2.pop(k) + jax.nn.sigmoid(gate_logits.pop(k)) * e[k]

    def mix(k):
        return (_dot(ypool_ref[sub_rows(k), :], wout_ref[0:pw, :])
                + _dot(ysb_ref[sub_rows(k), :], wout_ref[pw:, :]))

    norms(0, mix(0))
    mixes = [None] + [mix(k) for k in range(1, n_sub)]
    e = [_rms_norm(_dot(p_ref[sub_rows(k), :].astype(BF16), wple_ref[...]), gple_ref[...])
         for k in range(n_sub)]
    for k in range(n_sub):
        def between(k=k):
            if k + 1 < n_sub:
                norms(k + 1, mixes[k + 1])
            if k >= 1:
                post_norm(k - 1)
            if k >= 2:
                gate_dot(k - 2)
                emit(k - 2)
        mlp(k, between)
    if n_sub >= 2:
        gate_dot(n_sub - 2)
    post_norm(n_sub - 1)
    gate_dot(n_sub - 1)
    if n_sub >= 2:
        emit(n_sub - 2)
    emit(n_sub - 1)


def _post_mlp(x, ypool, ysb, p, wout, gpost, gpre, wup, wdown, gmlp, wgate, wple, gple):
    t, d = x.shape
    row_blk = lambda width: pl.BlockSpec((POST_ROW_TILE, width), lambda i: (i, 0))
    resident = lambda a: pl.BlockSpec(a.shape, lambda i: (0, 0), pipeline_mode=pl.Buffered(1))
    in_hbm = pl.BlockSpec(memory_space=pl.ANY)
    weights = (wout, wup, wdown, wgate, wple)
    assert all(w.shape[1] % d == 0 for w in weights)
    assert t % POST_ROW_TILE == 0 and POST_ROW_TILE % SUB_TILE == 0 and wup.shape[1] % d == 0
    return pl.pallas_call(
        _post_kernel,
        grid=(t // POST_ROW_TILE,),
        in_specs=[
            row_blk(d), row_blk(ypool.shape[1]), row_blk(ysb.shape[1]), row_blk(p.shape[1]),
            in_hbm, resident(gpost), resident(gpre), in_hbm, in_hbm,
            resident(gmlp), in_hbm, in_hbm, resident(gple),
        ],
        out_specs=row_blk(d),
        out_shape=jax.ShapeDtypeStruct((t, d), F32),
        scratch_shapes=[pltpu.VMEM(w.shape, BF16) for w in weights] + [
            pltpu.VMEM((WEIGHT_STAGE_SLOTS, WEIGHT_STAGE_ROWS, d), F32),
            pltpu.SemaphoreType.DMA((WEIGHT_STAGE_SLOTS,)),
        ],
        compiler_params=pltpu.CompilerParams(
            dimension_semantics=("arbitrary",),
            vmem_limit_bytes=60 * 1024 * 1024),
        name="post_mlp",
    )(x, ypool, ysb, p, wout, gpost, gpre, wup, wdown, gmlp, wgate, wple, gple)


@jax.jit
def kernel(x, p, g_mix_pre, w_in, w_pool, pool_scale, g_sb, w_out, g_mix_post, g_mlp_pre, w_up,
           w_down, g_mlp_post, w_ple_gate, w_ple_proj, g_ple):
    b, s, d = x.shape
    depth = w_in.shape[0]
    pw = pool_scale.shape[1]
    sw = g_sb.shape[1]
    row = lambda a: a.reshape(1, -1)
    h = x
    for i in range(depth):
        w = w_in[i].astype(BF16)
        wu, wq, wk, wv = w[:, :pw], w[:, pw:pw + sw], w[:, pw + sw:pw + 2 * sw], w[:, pw + 2 * sw:]
        ypool, q, kt, v = _inproj_pool(h, row(g_mix_pre[i]), wu, wq, wk.T, wv,
                                       w_pool[i].astype(BF16), row(pool_scale[i]))
        ysb = _sb_attention(q, kt, v, row(g_sb[i]))
        t = b * s
        h = _post_mlp(
            h.reshape(t, d), ypool.reshape(t, pw), ysb.reshape(t, sw), p[i].reshape(t, -1),
            w_out[i], row(g_mix_post[i]), row(g_mlp_pre[i]), w_up[i], w_down[i],
            row(g_mlp_post[i]), w_ple_gate[i], w_ple_proj[i], row(g_ple[i])).reshape(b, s, d)
    return h
```

```python
import jax
import jax.numpy as jnp
from jax import lax
from jax.experimental import pallas as pl
from jax.experimental.pallas import tpu as pltpu

F32 = jnp.float32
BF16 = jnp.bfloat16

EPS = 1e-6
POOL_WINDOWS = (2, 4, 8, 16)
HEAD_DIM = 64
Q_SCALE = HEAD_DIM ** -0.5
LOG2E = 1.4426950408889634

LANES = 128
HEADS_PER_TILE = LANES // HEAD_DIM

ROW_TILE = 1024
POST_ROW_TILE = 512
SUB_TILE = 256
WEIGHT_STAGE_ROWS = 256
WEIGHT_STAGE_SLOTS = 4
POOL_HALO = 16
TQ = 256
TK = 256
ROW_CHUNK = 128
PAIRS_PER_STEP = 4
PIPE_LAGS = (3, 3)
STICK_EXHAUSTED = 104.0
MASK_BIAS = -1e30

MIB = 1024 * 1024
INPROJ_VMEM_LIMIT = 40 * MIB
ATTN_VMEM_LIMIT = 48 * MIB
POST_VMEM_LIMIT = 60 * MIB

_NT_DIMS = (((1,), (1,)), ((), ()))


def _dot(a, b):
    return jnp.dot(a, b, preferred_element_type=F32)


def _rms_norm(x, g):
    ms = jnp.mean(x * x, axis=-1, keepdims=True)
    return x * lax.rsqrt(ms + EPS) * g


def _inproj_pool_kernel(x_ref, g_ref, wu_ref, wq_ref, wkt_ref, wv_ref, wpool_ref, pscale_ref,
                        ypool_ref, q_ref, kt_ref, v_ref, uext_ref):
    s = pl.program_id(1)
    rows = x_ref.shape[0]
    n_sub = rows // SUB_TILE
    sub_rows = lambda k: slice(k * SUB_TILE, (k + 1) * SUB_TILE)
    n_groups = len(POOL_WINDOWS)

    @pl.when(s == 0)
    def _():
        uext_ref[0:POOL_HALO, :] = jnp.zeros((POOL_HALO, uext_ref.shape[1]), F32)

    def normed(k):
        return _rms_norm(x_ref[sub_rows(k), :], g_ref[...]).astype(BF16)

    def project_u(k, hn):
        uext_ref[POOL_HALO + k * SUB_TILE:POOL_HALO + (k + 1) * SUB_TILE, :] = _dot(hn, wu_ref[...])

    def project_q(k, hn):
        q_ref[sub_rows(k), :] = (_dot(hn, wq_ref[...]) * Q_SCALE).astype(BF16)

    def project_v(k, hn):
        v_ref[sub_rows(k), :] = _dot(hn, wv_ref[...]).astype(BF16)

    def project_kt(k, hn):
        kt = lax.dot_general(wkt_ref[...], hn, _NT_DIMS, preferred_element_type=F32).astype(BF16)
        for hp in range(kt_ref.shape[0]):
            for kb in range(SUB_TILE // TK):
                kt_ref[hp, k * (SUB_TILE // TK) + kb] = kt[hp * LANES:(hp + 1) * LANES, kb * TK:(kb + 1) * TK]

    def pool(k, groups):
        base = POOL_HALO + k * SUB_TILE
        t = s * rows + k * SUB_TILE + lax.broadcasted_iota(jnp.int32, (SUB_TILE, 1), 0)
        for gi in groups:
            w = POOL_WINDOWS[gi]
            cols = slice(gi * LANES, (gi + 1) * LANES)
            ug = uext_ref[base:base + SUB_TILE, cols]
            acc = ug
            for i in range(1, w):
                acc = acc + uext_ref[base - i:base - i + SUB_TILE, cols]
            cnt = jnp.minimum(t + 1, w).astype(F32)
            d = acc / cnt - ug
            y = _dot(d.astype(BF16), wpool_ref[gi]) * pscale_ref[:, cols]
            ypool_ref[sub_rows(k), cols] = y.astype(BF16)

    hn = {0: normed(0)}
    project_u(0, hn[0])
    for k in range(n_sub):
        if k + 1 < n_sub:
            hn[k + 1] = normed(k + 1)
        project_q(k, hn[k])
        pool(k, range(0, n_groups // 2))
        if k + 1 < n_sub:
            project_u(k + 1, hn[k + 1])
        project_v(k, hn[k])
        pool(k, range(n_groups // 2, n_groups))
        project_kt(k, hn.pop(k))
    uext_ref[0:POOL_HALO, :] = uext_ref[rows:rows + POOL_HALO, :]


def _inproj_pool(x, g, wu, wq, wkt, wv, wpool, pscale):
    b, s, d = x.shape
    pw = wu.shape[1]
    sw = wq.shape[1]
    n_pairs = sw // LANES
    assert s % ROW_TILE == 0 and ROW_TILE % SUB_TILE == 0 and SUB_TILE % TK == 0
    assert pw == len(POOL_WINDOWS) * LANES and sw % LANES == 0 and max(POOL_WINDOWS) <= POOL_HALO
    kb_per_tile = ROW_TILE // TK
    const2 = lambda bi, si: (0, 0)
    row_blk = lambda width: pl.BlockSpec((None, ROW_TILE, width), lambda bi, si: (bi, si, 0))
    return pl.pallas_call(
        _inproj_pool_kernel,
        grid=(b, s // ROW_TILE),
        in_specs=[
            row_blk(d),
            pl.BlockSpec((1, d), const2),
            pl.BlockSpec((d, pw), const2),
            pl.BlockSpec((d, sw), const2),
            pl.BlockSpec((sw, d), const2),
            pl.BlockSpec((d, sw), const2),
            pl.BlockSpec(wpool.shape, lambda bi, si: (0, 0, 0)),
            pl.BlockSpec((1, pw), const2),
        ],
        out_specs=[
            row_blk(pw),
            row_blk(sw),
            pl.BlockSpec((None, n_pairs, kb_per_tile, LANES, TK), lambda bi, si: (bi, 0, si, 0, 0)),
            row_blk(sw),
        ],
        out_shape=[
            jax.ShapeDtypeStruct((b, s, pw), BF16),
            jax.ShapeDtypeStruct((b, s, sw), BF16),
            jax.ShapeDtypeStruct((b, n_pairs, s // TK, LANES, TK), BF16),
            jax.ShapeDtypeStruct((b, s, sw), BF16),
        ],
        scratch_shapes=[pltpu.VMEM((ROW_TILE + POOL_HALO, pw), F32)],
        compiler_params=pltpu.CompilerParams(
            dimension_semantics=("arbitrary", "arbitrary"),
            vmem_limit_bytes=INPROJ_VMEM_LIMIT),
        name="inproj_pool",
    )(x, g, wu, wq, wkt, wv, wpool, pscale)


def _softplus(z):
    return jnp.maximum(z, 0.0) + jnp.log(1.0 + jnp.exp2(jnp.abs(z) * (-LOG2E)))


def _cumulative_softplus(z, tri):
    return _dot(_softplus(z).astype(BF16), tri)


def _head_rms_norm(acc_first, acc_second, gain, first_head):
    o = jnp.where(first_head, acc_first, acc_second)
    o2 = o * o
    ms0 = jnp.sum(jnp.where(first_head, o2, 0.0), axis=-1, keepdims=True) * (1.0 / HEAD_DIM)
    ms1 = jnp.sum(jnp.where(first_head, 0.0, o2), axis=-1, keepdims=True) * (1.0 / HEAD_DIM)
    inv = jnp.where(first_head, lax.rsqrt(ms0 + EPS), lax.rsqrt(ms1 + EPS))
    return (o * inv * gain).astype(BF16)


def _attn_kernel(q_ref, kt_ref, v_ref, g_ref, tri_hbm, bias_hbm, o_ref,
                 tri_ref, bias_ref, const_sem, acc_ref, carry_ref):
    i = pl.program_id(2)
    n_pairs = kt_ref.shape[0]

    @pl.when((pl.program_id(0) == 0) & (pl.program_id(1) == 0) & (i == 0))
    def _():
        copies = (pltpu.make_async_copy(tri_hbm, tri_ref, const_sem.at[0]),
                  pltpu.make_async_copy(bias_hbm, bias_ref, const_sem.at[1]))
        for c in copies:
            c.start()
        for c in copies:
            c.wait()

    def tri(nk):
        return tri_ref[0:nk, 0:nk]

    first_head = lax.broadcasted_iota(jnp.int32, (ROW_CHUNK, LANES), 1) < HEAD_DIM
    pair_lanes = lambda p: slice(p * LANES, (p + 1) * LANES)

    def q_head(p, h, rows):
        qc = q_ref[rows, pair_lanes(p)]
        zero = jnp.zeros_like(qc)
        return jnp.where(first_head, qc, zero) if h == 0 else jnp.where(first_head, zero, qc)

    j_prev = jnp.maximum(i - 1, 0)
    no_prev = jnp.where(i >= 1, 0.0, -MASK_BIAS)
    units = [(p, r0, h, blk) for p in range(n_pairs) for r0 in range(0, TQ, ROW_CHUNK)
             for h in range(HEADS_PER_TILE) for blk in ("diag", "prev")]
    qm, z, cin, part, carry, acc = {}, {}, {}, {}, {}, {}

    def n_keys(u):
        return u[1] + ROW_CHUNK if u[3] == "diag" else TK - u[1]

    def scores(u):
        p, r0, h, blk = u
        rows = slice(r0, r0 + ROW_CHUNK)
        if blk == "diag":
            qm[p, r0, h] = q_head(p, h, rows)
            z[u] = _dot(qm[p, r0, h], kt_ref[p, i, :, 0:n_keys(u)]) + bias_ref[rows, 0:n_keys(u)]
        else:
            z[u] = _dot(qm[p, r0, h], kt_ref[p, j_prev, :, r0:TK])

    def stick(u):
        cin[u] = _cumulative_softplus(z[u], tri(n_keys(u)))

    def weights_times_values(u):
        p, r0, h, blk = u
        rows = slice(r0, r0 + ROW_CHUNK)
        if blk == "diag":
            a = jnp.exp(z.pop(u) - cin[u]).astype(BF16)
            part[p, r0, h] = _dot(a, v_ref[i, 0:n_keys(u), pair_lanes(p)])
            carry[p, r0, h] = cin.pop(u)[:, 0:1]
            return
        c_in = carry.pop((p, r0, h))
        a = jnp.exp(z.pop(u) - cin[u] - (c_in + no_prev)).astype(BF16)
        acc[p, r0, h] = part.pop((p, r0, h)) + _dot(a, v_ref[j_prev, r0:TK, pair_lanes(p)])
        acc_ref[p, h, rows] = acc[p, r0, h]
        carry_ref[p, h, rows] = c_in + cin.pop(u)[:, 0:1]
        if h == HEADS_PER_TILE - 1:
            o_ref[rows, pair_lanes(p)] = _head_rms_norm(
                acc.pop((p, r0, 0)), acc.pop((p, r0, 1)), g_ref[:, pair_lanes(p)], first_head)

    stages = ((scores, 0), (stick, PIPE_LAGS[0]), (weights_times_values, PIPE_LAGS[0] + PIPE_LAGS[1]))
    for t in range(len(units) + stages[-1][1]):
        for stage, delay in stages:
            if 0 <= t - delay < len(units):
                stage(units[t - delay])

    @pl.when(jnp.logical_and(i >= 1, jnp.min(carry_ref[...]) <= STICK_EXHAUSTED))
    def _():
        for p in range(n_pairs):
            for h in range(HEADS_PER_TILE):
                for r0 in range(ROW_CHUNK, TQ, ROW_CHUNK):
                    rows = slice(r0, r0 + ROW_CHUNK)
                    c_in = carry_ref[p, h, rows]
                    zj = _dot(q_head(p, h, rows), kt_ref[p, j_prev, :, 0:r0])
                    cj = _cumulative_softplus(zj, tri(r0))
                    a = jnp.exp(zj - cj - c_in).astype(BF16)
                    acc_ref[p, h, rows] += _dot(a, v_ref[j_prev, 0:r0, pair_lanes(p)])
                    carry_ref[p, h, rows] = c_in + cj[:, 0:1]

        def unfinished(state):
            j, min_carry = state
            return jnp.logical_and(j >= 0, min_carry <= STICK_EXHAUSTED)

        def step(state):
            j, _ = state
            for p in range(n_pairs):
                for h in range(HEADS_PER_TILE):
                    c_in = carry_ref[p, h]
                    qh = jnp.concatenate(
                        [q_head(p, h, slice(r0, r0 + ROW_CHUNK)) for r0 in range(0, TQ, ROW_CHUNK)], axis=0)
                    zj = _dot(qh, kt_ref[p, j])
                    cj = _cumulative_softplus(zj, tri(TK))
                    a = jnp.exp(zj - cj - c_in).astype(BF16)
                    acc_ref[p, h] += _dot(a, v_ref[j, :, pair_lanes(p)])
                    carry_ref[p, h] = c_in + cj[:, 0:1]
            return j - 1, jnp.min(carry_ref[...])

        lax.while_loop(unfinished, step, (i - 2, jnp.min(carry_ref[...])))
        for p in range(n_pairs):
            for r0 in range(0, TQ, ROW_CHUNK):
                rows = slice(r0, r0 + ROW_CHUNK)
                o_ref[rows, pair_lanes(p)] = _head_rms_norm(
                    acc_ref[p, 0, rows], acc_ref[p, 1, rows], g_ref[:, pair_lanes(p)], first_head)


def _sb_attention(q, kt, v, g_sb):
    b, s, sw = q.shape
    group_lanes = PAIRS_PER_STEP * LANES
    assert sw % group_lanes == 0 and s % TQ == 0 and TQ == TK and TQ % ROW_CHUNK == 0
    n_groups = sw // group_lanes
    n_kb = s // TK
    v_blocks = v.reshape(b, n_kb, TK, sw)
    r = lax.broadcasted_iota(jnp.int32, (TK, TK), 0)
    c = lax.broadcasted_iota(jnp.int32, (TK, TK), 1)
    tri = jnp.where(r >= c, 1.0, 0.0).astype(BF16)
    bias = jnp.where(c < r, 0.0, MASK_BIAS).astype(F32)
    return pl.pallas_call(
        _attn_kernel,
        grid=(b, n_groups, s // TQ),
        in_specs=[
            pl.BlockSpec((None, TQ, group_lanes), lambda bi, gi, i: (bi, i, gi)),
            pl.BlockSpec((None, PAIRS_PER_STEP, n_kb, LANES, TK), lambda bi, gi, i: (bi, gi, 0, 0, 0)),
            pl.BlockSpec((None, n_kb, TK, group_lanes), lambda bi, gi, i: (bi, 0, 0, gi)),
            pl.BlockSpec((1, group_lanes), lambda bi, gi, i: (0, gi)),
            pl.BlockSpec(memory_space=pl.ANY),
            pl.BlockSpec(memory_space=pl.ANY),
        ],
        out_specs=pl.BlockSpec((None, TQ, group_lanes), lambda bi, gi, i: (bi, i, gi)),
        out_shape=jax.ShapeDtypeStruct((b, s, sw), BF16),
        scratch_shapes=[
            pltpu.VMEM((TK, TK), BF16),
            pltpu.VMEM((TQ, TK), F32),
            pltpu.SemaphoreType.DMA((2,)),
            pltpu.VMEM((PAIRS_PER_STEP, HEADS_PER_TILE, TQ, LANES), F32),
            pltpu.VMEM((PAIRS_PER_STEP, HEADS_PER_TILE, TQ, 1), F32),
        ],
        compiler_params=pltpu.CompilerParams(
            dimension_semantics=("arbitrary", "arbitrary", "arbitrary"),
            vmem_limit_bytes=ATTN_VMEM_LIMIT),
        name="sb_attn",
    )(q, kt, v_blocks, g_sb, tri, bias)


def _load_weights_as_bf16(pairs, stage_ref, sem_ref):
    n_slots, stage_rows, stage_cols = stage_ref.shape
    chunks = []
    for src, dst in pairs:
        n_rows, n_cols = src.shape
        for r0 in range(0, n_rows, stage_rows):
            rows = min(stage_rows, n_rows - r0)
            for c0 in range(0, n_cols, stage_cols):
                chunks.append((src, dst, r0, rows, c0))

    def copy(k):
        src, _, r0, rows, c0 = chunks[k]
        slot = k % n_slots
        return pltpu.make_async_copy(
            src.at[r0:r0 + rows, c0:c0 + stage_cols], stage_ref.at[slot, 0:rows, :], sem_ref.at[slot])

    lookahead = n_slots - 1
    for k in range(min(lookahead, len(chunks))):
        copy(k).start()
    for k, (_, dst, r0, rows, c0) in enumerate(chunks):
        if k + lookahead < len(chunks):
            copy(k + lookahead).start()
        copy(k).wait()
        dst[r0:r0 + rows, c0:c0 + stage_cols] = stage_ref[k % n_slots, 0:rows, :].astype(BF16)


def _post_kernel(x_ref, ypool_ref, ysb_ref, p_ref, wout_hbm, gpost_ref, gpre_ref, wup_hbm, wdown_hbm,
                 gmlp_ref, wgate_hbm, wple_hbm, gple_ref, o_ref,
                 wout_ref, wup_ref, wdown_ref, wgate_ref, wple_ref, stage_ref, sem_ref):
    @pl.when(pl.program_id(0) == 0)
    def _():
        _load_weights_as_bf16(
            [(wout_hbm, wout_ref), (wup_hbm, wup_ref), (wdown_hbm, wdown_ref), (wgate_hbm, wgate_ref),
             (wple_hbm, wple_ref)], stage_ref, sem_ref)

    pw = ypool_ref.shape[1]
    d_model = wup_ref.shape[0]
    n_sub = x_ref.shape[0] // SUB_TILE
    sub_rows = lambda k: slice(k * SUB_TILE, (k + 1) * SUB_TILE)
    h, hn, m = {}, {}, {}

    def norms(k, mix):
        h[k] = x_ref[sub_rows(k), :] + _rms_norm(mix, gpost_ref[...])
        hn[k] = _rms_norm(h[k], gpre_ref[...]).astype(BF16)

    def mlp(k, after_first_up):
        acc = None
        for c0 in range(0, wup_ref.shape[1], d_model):
            up = _dot(hn[k], wup_ref[:, c0:c0 + d_model])
            if c0 == 0:
                after_first_up()
            act = jnp.square(jnp.maximum(up, 0.0)).astype(BF16)
            part = _dot(act, wdown_ref[c0:c0 + d_model, :])
            acc = part if acc is None else acc + part
        m[k] = acc

    h2, gate_logits = {}, {}

    def post_norm(k):
        h2[k] = h.pop(k) + _rms_norm(m.pop(k), gmlp_ref[...])

    def gate_dot(k):
        gate_logits[k] = _dot(h2[k].astype(BF16), wgate_ref[...])

    def emit(k):
        o_ref[sub_rows(k), :] = h2.pop(k) + jax.nn.sigmoid(gate_logits.pop(k)) * e[k]

    def mix(k):
        return (_dot(ypool_ref[sub_rows(k), :], wout_ref[0:pw, :])
                + _dot(ysb_ref[sub_rows(k), :], wout_ref[pw:, :]))

    norms(0, mix(0))
    mixes = [None] + [mix(k) for k in range(1, n_sub)]
    e = [_rms_norm(_dot(p_ref[sub_rows(k), :].astype(BF16), wple_ref[...]), gple_ref[...])
         for k in range(n_sub)]
    for k in range(n_sub):
        def between(k=k):
            if k + 1 < n_sub:
                norms(k + 1, mixes[k + 1])
            if k >= 1:
                post_norm(k - 1)
            if k >= 2:
                gate_dot(k - 2)
                emit(k - 2)
        mlp(k, between)
    if n_sub >= 2:
        gate_dot(n_sub - 2)
    post_norm(n_sub - 1)
    gate_dot(n_sub - 1)
    if n_sub >= 2:
        emit(n_sub - 2)
    emit(n_sub - 1)


def _post_mlp(x, ypool, ysb, p, wout, gpost, gpre, wup, wdown, gmlp, wgate, wple, gple):
    t, d = x.shape
    row_blk = lambda width: pl.BlockSpec((POST_ROW_TILE, width), lambda i: (i, 0))
    resident = lambda a: pl.BlockSpec(a.shape, lambda i: (0, 0), pipeline_mode=pl.Buffered(1))
    in_hbm = pl.BlockSpec(memory_space=pl.ANY)
    weights = (wout, wup, wdown, wgate, wple)
    assert all(w.shape[1] % d == 0 for w in weights)
    assert t % POST_ROW_TILE == 0 and POST_ROW_TILE % SUB_TILE == 0 and wup.shape[1] % d == 0
    return pl.pallas_call(
        _post_kernel,
        grid=(t // POST_ROW_TILE,),
        in_specs=[
            row_blk(d), row_blk(ypool.shape[1]), row_blk(ysb.shape[1]), row_blk(p.shape[1]),
            in_hbm, resident(gpost), resident(gpre), in_hbm, in_hbm,
            resident(gmlp), in_hbm, in_hbm, resident(gple),
        ],
        out_specs=row_blk(d),
        out_shape=jax.ShapeDtypeStruct((t, d), F32),
        scratch_shapes=[pltpu.VMEM(w.shape, BF16) for w in weights] + [
            pltpu.VMEM((WEIGHT_STAGE_SLOTS, WEIGHT_STAGE_ROWS, d), F32),
            pltpu.SemaphoreType.DMA((WEIGHT_STAGE_SLOTS,)),
        ],
        compiler_params=pltpu.CompilerParams(
            dimension_semantics=("arbitrary",),
            vmem_limit_bytes=POST_VMEM_LIMIT),
        name="post_mlp",
    )(x, ypool, ysb, p, wout, gpost, gpre, wup, wdown, gmlp, wgate, wple, gple)


@jax.jit
def kernel(x, p, g_mix_pre, w_in, w_pool, pool_scale, g_sb, w_out, g_mix_post, g_mlp_pre, w_up,
           w_down, g_mlp_post, w_ple_gate, w_ple_proj, g_ple):
    b, s, d = x.shape
    depth = w_in.shape[0]
    pw = pool_scale.shape[1]
    sw = g_sb.shape[1]
    row = lambda a: a.reshape(1, -1)
    h = x
    for i in range(depth):
        w = w_in[i].astype(BF16)
        wu, wq, wk, wv = w[:, :pw], w[:, pw:pw + sw], w[:, pw + sw:pw + 2 * sw], w[:, pw + 2 * sw:]
        ypool, q, kt, v = _inproj_pool(h, row(g_mix_pre[i]), wu, wq, wk.T, wv,
                                       w_pool[i].astype(BF16), row(pool_scale[i]))
        ysb = _sb_attention(q, kt, v, row(g_sb[i]))
        t = b * s
        h = _post_mlp(
            h.reshape(t, d), ypool.reshape(t, pw), ysb.reshape(t, sw), p[i].reshape(t, -1),
            w_out[i], row(g_mix_post[i]), row(g_mlp_pre[i]), w_up[i], w_down[i],
            row(g_mlp_post[i]), w_ple_gate[i], w_ple_proj[i], row(g_ple[i])).reshape(b, s, d)
    return h
```

```python
import jax
import jax.numpy as jnp
from jax import lax
from jax.experimental import pallas as pl
from jax.experimental.pallas import tpu as pltpu

F32 = jnp.float32
BF16 = jnp.bfloat16

EPS = 1e-6
POOL_WINDOWS = (2, 4, 8, 16)
HEAD_DIM = 64
Q_SCALE = HEAD_DIM ** -0.5
LOG2E = 1.4426950408889634

LANES = 128
HEADS_PER_TILE = LANES // HEAD_DIM

ROW_TILE = 1024
POST_ROW_TILE = 512
SUB_TILE = 256
WEIGHT_STAGE_ROWS = 256
WEIGHT_STAGE_SLOTS = 4
POOL_HALO = 16
TQ = 256
TK = 256
ROW_CHUNK = 128
PAIRS_PER_STEP = 4
PIPE_LAGS = (3, 3)
STICK_EXHAUSTED = 104.0
MASK_BIAS = -1e30

MIB = 1024 * 1024
INPROJ_VMEM_LIMIT = 40 * MIB
ATTN_VMEM_LIMIT = 48 * MIB
POST_VMEM_LIMIT = 60 * MIB

_NT_DIMS = (((1,), (1,)), ((), ()))


def _dot(a, b):
    return jnp.dot(a, b, preferred_element_type=F32)


def _rms_norm(x, g):
    ms = jnp.mean(x * x, axis=-1, keepdims=True)
    return x * lax.rsqrt(ms + EPS) * g


def _inproj_pool_kernel(x_ref, g_ref, wu_ref, wq_ref, wkt_ref, wv_ref, wpool_ref, pscale_ref,
                        ypool_ref, q_ref, kt_ref, v_ref, uext_ref):
    s = pl.program_id(1)
    rows = x_ref.shape[0]
    n_sub = rows // SUB_TILE
    sub_rows = lambda k: slice(k * SUB_TILE, (k + 1) * SUB_TILE)
    n_groups = len(POOL_WINDOWS)

    @pl.when(s == 0)
    def _():
        uext_ref[0:POOL_HALO, :] = jnp.zeros((POOL_HALO, uext_ref.shape[1]), F32)

    def normed(k):
        return _rms_norm(x_ref[sub_rows(k), :], g_ref[...]).astype(BF16)

    def project_u(k, hn):
        uext_ref[POOL_HALO + k * SUB_TILE:POOL_HALO + (k + 1) * SUB_TILE, :] = _dot(hn, wu_ref[...])

    def project_q(k, hn):
        q_ref[sub_rows(k), :] = (_dot(hn, wq_ref[...]) * Q_SCALE).astype(BF16)

    def project_v(k, hn):
        v_ref[sub_rows(k), :] = _dot(hn, wv_ref[...]).astype(BF16)

    def project_kt(k, hn):
        kt = lax.dot_general(wkt_ref[...], hn, _NT_DIMS, preferred_element_type=F32).astype(BF16)
        for hp in range(kt_ref.shape[0]):
            for kb in range(SUB_TILE // TK):
                kt_ref[hp, k * (SUB_TILE // TK) + kb] = kt[hp * LANES:(hp + 1) * LANES, kb * TK:(kb + 1) * TK]

    def pool(k, groups):
        base = POOL_HALO + k * SUB_TILE
        t = s * rows + k * SUB_TILE + lax.broadcasted_iota(jnp.int32, (SUB_TILE, 1), 0)
        for gi in groups:
            w = POOL_WINDOWS[gi]
            cols = slice(gi * LANES, (gi + 1) * LANES)
            ug = uext_ref[base:base + SUB_TILE, cols]
            acc = ug
            for i in range(1, w):
                acc = acc + uext_ref[base - i:base - i + SUB_TILE, cols]
            cnt = jnp.minimum(t + 1, w).astype(F32)
            d = acc / cnt - ug
            y = _dot(d.astype(BF16), wpool_ref[gi]) * pscale_ref[:, cols]
            ypool_ref[sub_rows(k), cols] = y.astype(BF16)

    hn = {0: normed(0)}
    project_u(0, hn[0])
    for k in range(n_sub):
        if k + 1 < n_sub:
            hn[k + 1] = normed(k + 1)
        project_q(k, hn[k])
        pool(k, range(0, n_groups // 2))
        if k + 1 < n_sub:
            project_u(k + 1, hn[k + 1])
        project_v(k, hn[k])
        pool(k, range(n_groups // 2, n_groups))
        project_kt(k, hn.pop(k))
    uext_ref[0:POOL_HALO, :] = uext_ref[rows:rows + POOL_HALO, :]


def _inproj_pool(x, g, wu, wq, wkt, wv, wpool, pscale):
    b, s, d = x.shape
    pw = wu.shape[1]
    sw = wq.shape[1]
    n_pairs = sw // LANES
    assert s % ROW_TILE == 0 and ROW_TILE % SUB_TILE == 0 and SUB_TILE % TK == 0
    assert pw == len(POOL_WINDOWS) * LANES and sw % LANES == 0 and max(POOL_WINDOWS) <= POOL_HALO
    kb_per_tile = ROW_TILE // TK
    const2 = lambda bi, si: (0, 0)
    row_blk = lambda width: pl.BlockSpec((None, ROW_TILE, width), lambda bi, si: (bi, si, 0))
    return pl.pallas_call(
        _inproj_pool_kernel,
        grid=(b, s // ROW_TILE),
        in_specs=[
            row_blk(d),
            pl.BlockSpec((1, d), const2),
            pl.BlockSpec((d, pw), const2),
            pl.BlockSpec((d, sw), const2),
            pl.BlockSpec((sw, d), const2),
            pl.BlockSpec((d, sw), const2),
            pl.BlockSpec(wpool.shape, lambda bi, si: (0, 0, 0)),
            pl.BlockSpec((1, pw), const2),
        ],
        out_specs=[
            row_blk(pw),
            row_blk(sw),
            pl.BlockSpec((None, n_pairs, kb_per_tile, LANES, TK), lambda bi, si: (bi, 0, si, 0, 0)),
            row_blk(sw),
        ],
        out_shape=[
            jax.ShapeDtypeStruct((b, s, pw), BF16),
            jax.ShapeDtypeStruct((b, s, sw), BF16),
            jax.ShapeDtypeStruct((b, n_pairs, s // TK, LANES, TK), BF16),
            jax.ShapeDtypeStruct((b, s, sw), BF16),
        ],
        scratch_shapes=[pltpu.VMEM((ROW_TILE + POOL_HALO, pw), F32)],
        compiler_params=pltpu.CompilerParams(
            dimension_semantics=("arbitrary", "arbitrary"),
            vmem_limit_bytes=INPROJ_VMEM_LIMIT),
        name="inproj_pool",
    )(x, g, wu, wq, wkt, wv, wpool, pscale)


def _softplus(z):
    return jnp.maximum(z, 0.0) + jnp.log(1.0 + jnp.exp2(jnp.abs(z) * (-LOG2E)))


def _cumulative_softplus(z, tri):
    return _dot(_softplus(z).astype(BF16), tri)


def _head_rms_norm(o, gain, first_head):
    o2 = o * o
    ms0 = jnp.sum(jnp.where(first_head, o2, 0.0), axis=-1, keepdims=True) * (1.0 / HEAD_DIM)
    ms1 = jnp.sum(jnp.where(first_head, 0.0, o2), axis=-1, keepdims=True) * (1.0 / HEAD_DIM)
    inv = jnp.where(first_head, lax.rsqrt(ms0 + EPS), lax.rsqrt(ms1 + EPS))
    return (o * inv * gain).astype(BF16)


def _attn_kernel(q_ref, kt_ref, v_ref, tri_hbm, bias_hbm, o_ref,
                 tri_ref, bias_ref, const_sem, acc_ref, carry_ref):
    i = pl.program_id(2)
    n_pairs = kt_ref.shape[0]

    @pl.when((pl.program_id(0) == 0) & (pl.program_id(1) == 0) & (i == 0))
    def _():
        copies = (pltpu.make_async_copy(tri_hbm, tri_ref, const_sem.at[0]),
                  pltpu.make_async_copy(bias_hbm, bias_ref, const_sem.at[1]))
        for c in copies:
            c.start()
        for c in copies:
            c.wait()

    def tri(nk):
        return tri_ref[0:nk, 0:nk]

    first_head = lax.broadcasted_iota(jnp.int32, (ROW_CHUNK, LANES), 1) < HEAD_DIM
    pair_lanes = lambda p: slice(p * LANES, (p + 1) * LANES)

    def q_head(p, h, rows):
        qc = q_ref[rows, pair_lanes(p)]
        zero = jnp.zeros_like(qc)
        return jnp.where(first_head, qc, zero) if h == 0 else jnp.where(first_head, zero, qc)

    j_prev = jnp.maximum(i - 1, 0)
    no_prev = jnp.where(i >= 1, 0.0, -MASK_BIAS)
    units = [(p, r0, h, blk) for p in range(n_pairs) for r0 in range(0, TQ, ROW_CHUNK)
             for h in range(HEADS_PER_TILE) for blk in ("diag", "prev")]
    qm, z, cin, part, carry, acc = {}, {}, {}, {}, {}, {}

    def n_keys(u):
        return u[1] + ROW_CHUNK if u[3] == "diag" else TK - u[1]

    def scores(u):
        p, r0, h, blk = u
        rows = slice(r0, r0 + ROW_CHUNK)
        if blk == "diag":
            qm[p, r0, h] = q_head(p, h, rows)
            z[u] = _dot(qm[p, r0, h], kt_ref[p, i, :, 0:n_keys(u)]) + bias_ref[rows, 0:n_keys(u)]
        else:
            z[u] = _dot(qm[p, r0, h], kt_ref[p, j_prev, :, r0:TK])

    def stick(u):
        cin[u] = _cumulative_softplus(z[u], tri(n_keys(u)))

    def weights_times_values(u):
        p, r0, h, blk = u
        rows = slice(r0, r0 + ROW_CHUNK)
        if blk == "diag":
            a = jnp.exp(z.pop(u) - cin[u]).astype(BF16)
            part[p, r0, h] = _dot(a, v_ref[i, 0:n_keys(u), pair_lanes(p)])
            carry[p, r0, h] = cin.pop(u)[:, 0:1]
            return
        c_in = carry.pop((p, r0, h))
        a = jnp.exp(z.pop(u) - cin[u] - (c_in + no_prev)).astype(BF16)
        acc[p, r0, h] = part.pop((p, r0, h)) + _dot(a, v_ref[j_prev, r0:TK, pair_lanes(p)])
        acc_ref[p, h, rows] = acc[p, r0, h]
        carry_ref[p, h, rows] = c_in + cin.pop(u)[:, 0:1]
        if h == HEADS_PER_TILE - 1:
            o_ref[rows, pair_lanes(p)] = jnp.where(first_head, acc.pop((p, r0, 0)), acc.pop((p, r0, 1)))

    stages = ((scores, 0), (stick, PIPE_LAGS[0]), (weights_times_values, PIPE_LAGS[0] + PIPE_LAGS[1]))
    for t in range(len(units) + stages[-1][1]):
        for stage, delay in stages:
            if 0 <= t - delay < len(units):
                stage(units[t - delay])

    @pl.when(jnp.logical_and(i >= 1, jnp.min(carry_ref[...]) <= STICK_EXHAUSTED))
    def _():
        for p in range(n_pairs):
            for h in range(HEADS_PER_TILE):
                for r0 in range(ROW_CHUNK, TQ, ROW_CHUNK):
                    rows = slice(r0, r0 + ROW_CHUNK)
                    c_in = carry_ref[p, h, rows]
                    zj = _dot(q_head(p, h, rows), kt_ref[p, j_prev, :, 0:r0])
                    cj = _cumulative_softplus(zj, tri(r0))
                    a = jnp.exp(zj - cj - c_in).astype(BF16)
                    acc_ref[p, h, rows] += _dot(a, v_ref[j_prev, 0:r0, pair_lanes(p)])
                    carry_ref[p, h, rows] = c_in + cj[:, 0:1]

        def unfinished(state):
            j, min_carry = state
            return jnp.logical_and(j >= 0, min_carry <= STICK_EXHAUSTED)

        def step(state):
            j, _ = state
            for p in range(n_pairs):
                for h in range(HEADS_PER_TILE):
                    c_in = carry_ref[p, h]
                    qh = jnp.concatenate(
                        [q_head(p, h, slice(r0, r0 + ROW_CHUNK)) for r0 in range(0, TQ, ROW_CHUNK)], axis=0)
                    zj = _dot(qh, kt_ref[p, j])
                    cj = _cumulative_softplus(zj, tri(TK))
                    a = jnp.exp(zj - cj - c_in).astype(BF16)
                    acc_ref[p, h] += _dot(a, v_ref[j, :, pair_lanes(p)])
                    carry_ref[p, h] = c_in + cj[:, 0:1]
            return j - 1, jnp.min(carry_ref[...])

        lax.while_loop(unfinished, step, (i - 2, jnp.min(carry_ref[...])))
        for p in range(n_pairs):
            for r0 in range(0, TQ, ROW_CHUNK):
                rows = slice(r0, r0 + ROW_CHUNK)
                o_ref[rows, pair_lanes(p)] = jnp.where(first_head, acc_ref[p, 0, rows], acc_ref[p, 1, rows])


def _sb_attention(q, kt, v):
    b, s, sw = q.shape
    group_lanes = PAIRS_PER_STEP * LANES
    assert sw % group_lanes == 0 and s % TQ == 0 and TQ == TK and TQ % ROW_CHUNK == 0
    n_groups = sw // group_lanes
    n_kb = s // TK
    v_blocks = v.reshape(b, n_kb, TK, sw)
    r = lax.broadcasted_iota(jnp.int32, (TK, TK), 0)
    c = lax.broadcasted_iota(jnp.int32, (TK, TK), 1)
    tri = jnp.where(r >= c, 1.0, 0.0).astype(BF16)
    bias = jnp.where(c < r, 0.0, MASK_BIAS).astype(F32)
    return pl.pallas_call(
        _attn_kernel,
        grid=(b, n_groups, s // TQ),
        in_specs=[
            pl.BlockSpec((None, TQ, group_lanes), lambda bi, gi, i: (bi, i, gi)),
            pl.BlockSpec((None, PAIRS_PER_STEP, n_kb, LANES, TK), lambda bi, gi, i: (bi, gi, 0, 0, 0)),
            pl.BlockSpec((None, n_kb, TK, group_lanes), lambda bi, gi, i: (bi, 0, 0, gi)),
            pl.BlockSpec(memory_space=pl.ANY),
            pl.BlockSpec(memory_space=pl.ANY),
        ],
        out_specs=pl.BlockSpec((None, TQ, group_lanes), lambda bi, gi, i: (bi, i, gi)),
        out_shape=jax.ShapeDtypeStruct((b, s, sw), F32),
        scratch_shapes=[
            pltpu.VMEM((TK, TK), BF16),
            pltpu.VMEM((TQ, TK), F32),
            pltpu.SemaphoreType.DMA((2,)),
            pltpu.VMEM((PAIRS_PER_STEP, HEADS_PER_TILE, TQ, LANES), F32),
            pltpu.VMEM((PAIRS_PER_STEP, HEADS_PER_TILE, TQ, 1), F32),
        ],
        compiler_params=pltpu.CompilerParams(
            dimension_semantics=("arbitrary", "arbitrary", "arbitrary"),
            vmem_limit_bytes=ATTN_VMEM_LIMIT),
        name="sb_attn",
    )(q, kt, v_blocks, tri, bias)


def _load_weights_as_bf16(pairs, stage_ref, sem_ref):
    n_slots, stage_rows, stage_cols = stage_ref.shape
    chunks = []
    for src, dst in pairs:
        n_rows, n_cols = src.shape
        for r0 in range(0, n_rows, stage_rows):
            rows = min(stage_rows, n_rows - r0)
            for c0 in range(0, n_cols, stage_cols):
                chunks.append((src, dst, r0, rows, c0))

    def copy(k):
        src, _, r0, rows, c0 = chunks[k]
        slot = k % n_slots
        return pltpu.make_async_copy(
            src.at[r0:r0 + rows, c0:c0 + stage_cols], stage_ref.at[slot, 0:rows, :], sem_ref.at[slot])

    lookahead = n_slots - 1
    for k in range(min(lookahead, len(chunks))):
        copy(k).start()
    for k, (_, dst, r0, rows, c0) in enumerate(chunks):
        if k + lookahead < len(chunks):
            copy(k + lookahead).start()
        copy(k).wait()
        dst[r0:r0 + rows, c0:c0 + stage_cols] = stage_ref[k % n_slots, 0:rows, :].astype(BF16)


def _post_kernel(x_ref, ypool_ref, ysb_ref, p_ref, wout_hbm, gpost_ref, gpre_ref, wup_hbm, wdown_hbm,
                 gmlp_ref, wgate_hbm, wple_hbm, gple_ref, gsb_ref, o_ref,
                 wout_ref, wup_ref, wdown_ref, wgate_ref, wple_ref, stage_ref, sem_ref):
    @pl.when(pl.program_id(0) == 0)
    def _():
        _load_weights_as_bf16(
            [(wout_hbm, wout_ref), (wup_hbm, wup_ref), (wdown_hbm, wdown_ref), (wgate_hbm, wgate_ref),
             (wple_hbm, wple_ref)], stage_ref, sem_ref)

    pw = ypool_ref.shape[1]
    d_model = wup_ref.shape[0]
    n_sub = x_ref.shape[0] // SUB_TILE
    sub_rows = lambda k: slice(k * SUB_TILE, (k + 1) * SUB_TILE)
    h, hn, m = {}, {}, {}

    def norms(k, mix):
        h[k] = x_ref[sub_rows(k), :] + _rms_norm(mix, gpost_ref[...])
        hn[k] = _rms_norm(h[k], gpre_ref[...]).astype(BF16)

    def mlp(k, after_first_up):
        acc = None
        for c0 in range(0, wup_ref.shape[1], d_model):
            up = _dot(hn[k], wup_ref[:, c0:c0 + d_model])
            if c0 == 0:
                after_first_up()
            act = jnp.square(jnp.maximum(up, 0.0)).astype(BF16)
            part = _dot(act, wdown_ref[c0:c0 + d_model, :])
            acc = part if acc is None else acc + part
        m[k] = acc

    h2, gate_logits = {}, {}

    def post_norm(k):
        h2[k] = h.pop(k) + _rms_norm(m.pop(k), gmlp_ref[...])

    def gate_dot(k):
        gate_logits[k] = _dot(h2[k].astype(BF16), wgate_ref[...])

    def emit(k):
        o_ref[sub_rows(k), :] = h2.pop(k) + jax.nn.sigmoid(gate_logits.pop(k)) * e[k]

    first_head = lax.broadcasted_iota(jnp.int32, (SUB_TILE, LANES), 1) < HEAD_DIM

    def mix(k):
        pooled = _dot(ypool_ref[sub_rows(k), :], wout_ref[0:pw, :])
        ysb = jnp.concatenate(
            [_head_rms_norm(ysb_ref[sub_rows(k), c0:c0 + LANES], gsb_ref[:, c0:c0 + LANES], first_head)
             for c0 in range(0, ysb_ref.shape[1], LANES)], axis=1)
        return pooled + _dot(ysb, wout_ref[pw:, :])

    norms(0, mix(0))
    mixes = [None] + [mix(k) for k in range(1, n_sub)]
    e = [_rms_norm(_dot(p_ref[sub_rows(k), :].astype(BF16), wple_ref[...]), gple_ref[...])
         for k in range(n_sub)]
    for k in range(n_sub):
        def between(k=k):
            if k + 1 < n_sub:
                norms(k + 1, mixes[k + 1])
            if k >= 1:
                post_norm(k - 1)
            if k >= 2:
                gate_dot(k - 2)
                emit(k - 2)
        mlp(k, between)
    if n_sub >= 2:
        gate_dot(n_sub - 2)
    post_norm(n_sub - 1)
    gate_dot(n_sub - 1)
    if n_sub >= 2:
        emit(n_sub - 2)
    emit(n_sub - 1)


def _post_mlp(x, ypool, ysb, p, wout, gpost, gpre, wup, wdown, gmlp, wgate, wple, gple, gsb):
    t, d = x.shape
    row_blk = lambda width: pl.BlockSpec((POST_ROW_TILE, width), lambda i: (i, 0))
    resident = lambda a: pl.BlockSpec(a.shape, lambda i: (0, 0), pipeline_mode=pl.Buffered(1))
    in_hbm = pl.BlockSpec(memory_space=pl.ANY)
    weights = (wout, wup, wdown, wgate, wple)
    assert all(w.shape[1] % d == 0 for w in weights)
    assert t % POST_ROW_TILE == 0 and POST_ROW_TILE % SUB_TILE == 0 and wup.shape[1] % d == 0
    return pl.pallas_call(
        _post_kernel,
        grid=(t // POST_ROW_TILE,),
        in_specs=[
            row_blk(d), row_blk(ypool.shape[1]), row_blk(ysb.shape[1]), row_blk(p.shape[1]),
            in_hbm, resident(gpost), resident(gpre), in_hbm, in_hbm,
            resident(gmlp), in_hbm, in_hbm, resident(gple), resident(gsb),
        ],
        out_specs=row_blk(d),
        out_shape=jax.ShapeDtypeStruct((t, d), F32),
        scratch_shapes=[pltpu.VMEM(w.shape, BF16) for w in weights] + [
            pltpu.VMEM((WEIGHT_STAGE_SLOTS, WEIGHT_STAGE_ROWS, d), F32),
            pltpu.SemaphoreType.DMA((WEIGHT_STAGE_SLOTS,)),
        ],
        compiler_params=pltpu.CompilerParams(
            dimension_semantics=("arbitrary",),
            vmem_limit_bytes=POST_VMEM_LIMIT),
        name="post_mlp",
    )(x, ypool, ysb, p, wout, gpost, gpre, wup, wdown, gmlp, wgate, wple, gple, gsb)


@jax.jit
def kernel(x, p, g_mix_pre, w_in, w_pool, pool_scale, g_sb, w_out, g_mix_post, g_mlp_pre, w_up,
           w_down, g_mlp_post, w_ple_gate, w_ple_proj, g_ple):
    b, s, d = x.shape
    depth = w_in.shape[0]
    pw = pool_scale.shape[1]
    sw = g_sb.shape[1]
    row = lambda a: a.reshape(1, -1)
    h = x
    for i in range(depth):
        w = w_in[i].astype(BF16)
        wu, wq, wk, wv = w[:, :pw], w[:, pw:pw + sw], w[:, pw + sw:pw + 2 * sw], w[:, pw + 2 * sw:]
        ypool, q, kt, v = _inproj_pool(h, row(g_mix_pre[i]), wu, wq, wk.T, wv,
                                       w_pool[i].astype(BF16), row(pool_scale[i]))
        ysb = _sb_attention(q, kt, v)
        t = b * s
        h = _post_mlp(
            h.reshape(t, d), ypool.reshape(t, pw), ysb.reshape(t, sw), p[i].reshape(t, -1),
            w_out[i], row(g_mix_post[i]), row(g_mlp_pre[i]), w_up[i], w_down[i],
            row(g_mlp_post[i]), w_ple_gate[i], w_ple_proj[i], row(g_ple[i]),
            row(g_sb[i])).reshape(b, s, d)
    return h
```
